```python
import math
import jax, jax.numpy as jnp
from jax import lax
import numpy as np

D_MODEL = 2048
BATCH = 2
SEQ = 4096
DEPTH = 4

BRANCH_WIDTH = 1024
N_BRANCH = 3
GLA_HEADS = 4
GLA_DK = 128
GLA_DV = 256
GLA_RANK = 16
GLA_TAU = 16.0
GLA_CHUNK = 64
SWA_Q_HEADS = 16
SWA_KV_HEADS = 2
SWA_HEAD_DIM = 64
SWA_WINDOW = 128
MOBA_HEADS = 8
MOBA_HEAD_DIM = 128
MOBA_BLOCK = 256
MOBA_TOPK = 3
MOBA_Q_CHUNK = 16
DEEPNORM_ALPHA = (2 * DEPTH) ** 0.25
DEEPNORM_BETA = (8 * DEPTH) ** -0.25
LN_EPS = 1e-5
RMS_EPS = 1e-6

IN_SPLITS = (
    GLA_HEADS * GLA_DK, GLA_HEADS * GLA_DK, GLA_HEADS * GLA_DV, GLA_RANK, BRANCH_WIDTH,
    SWA_Q_HEADS * SWA_HEAD_DIM, SWA_KV_HEADS * SWA_HEAD_DIM, SWA_KV_HEADS * SWA_HEAD_DIM, BRANCH_WIDTH,
    MOBA_HEADS * MOBA_HEAD_DIM, MOBA_HEADS * MOBA_HEAD_DIM, MOBA_HEADS * MOBA_HEAD_DIM, BRANCH_WIDTH,
    N_BRANCH * D_MODEL,
)
VALUE_SEGMENTS = (2, 7, 11)
D_IN = sum(IN_SPLITS)

kernel_name = "hybrid_gla_swa_moba_block"


def _layer_norm(h, g, b):
    h32 = h.astype(jnp.float32)
    mu = h32.mean(-1, keepdims=True)
    var = jnp.square(h32 - mu).mean(-1, keepdims=True)
    return ((h32 - mu) * lax.rsqrt(var + LN_EPS) * g.astype(jnp.float32) + b.astype(jnp.float32)).astype(h.dtype)


def _gla_branch(q, k, v, g, norm_g):
    B, S, H, dk = q.shape
    dv = v.shape[-1]
    nc = S // GLA_CHUNK

    def to_chunks(t):
        t = t.astype(jnp.float32).reshape(B, nc, GLA_CHUNK, H, t.shape[-1])
        return t.transpose(1, 0, 3, 2, 4)

    qc, kc, vc, gc = to_chunks(q * (dk ** -0.5)), to_chunks(k), to_chunks(v), to_chunks(g)
    causal = jnp.tril(jnp.ones((GLA_CHUNK, GLA_CHUNK), dtype=bool))

    def step(state, inp):
        qi, ki, vi, gi = inp
        b = jnp.cumsum(gi, axis=2)
        b_last = b[:, :, -1:, :]
        o_inter = jnp.einsum('bhcd,bhde->bhce', qi * jnp.exp(b), state)
        diff = b[:, :, :, None, :] - b[:, :, None, :, :]
        decay = jnp.exp(jnp.where(causal[None, None, :, :, None], diff, -jnp.inf))
        attn = jnp.einsum('bhid,bhjd,bhijd->bhij', qi, ki, decay)
        o = o_inter + jnp.einsum('bhij,bhje->bhie', attn, vi)
        new_state = (jnp.exp(b_last[:, :, 0, :])[..., None] * state
                     + jnp.einsum('bhcd,bhce->bhde', ki * jnp.exp(b_last - b), vi))
        return new_state, o

    state0 = jnp.zeros((B, H, dk, dv), jnp.float32)
    _, o = lax.scan(step, state0, (qc, kc, vc, gc))
    o = o.transpose(1, 0, 3, 2, 4).reshape(B, S, H, dv)
    o = o * lax.rsqrt(jnp.mean(jnp.square(o), axis=-1, keepdims=True) + RMS_EPS) * norm_g.astype(jnp.float32)
    return o.reshape(B, S, H * dv).astype(v.dtype)


def _swa_branch(q, k, v, sinks):
    B, S, Hq, hd = q.shape
    Hkv = k.shape[2]
    G = Hq // Hkv
    W = SWA_WINDOW
    nb = S // W
    qb = q.reshape(B, nb, W, Hkv, G, hd)
    kb = k.reshape(B, nb, W, Hkv, hd)
    vb = v.reshape(B, nb, W, Hkv, hd)
    shift = lambda t: jnp.concatenate([jnp.zeros_like(t[:, :1]), t[:, :-1]], axis=1)
    kk = jnp.concatenate([shift(kb), kb], axis=2)
    vv = jnp.concatenate([shift(vb), vb], axis=2)
    s = jnp.einsum('bnqhgd,bnkhd->bnhgqk', qb, kk).astype(jnp.float32) * (hd ** -0.5)
    qpos = jnp.arange(W)[:, None] + W
    kpos = jnp.arange(2 * W)[None, :]
    rel = qpos - kpos
    band = (rel >= 0) & (rel < W)
    mask = band[None] & ((jnp.arange(nb) > 0)[:, None, None] | (kpos >= W)[None])
    s = jnp.where(mask[None, :, None, None], s, -jnp.inf)
    sink = jnp.broadcast_to(sinks.astype(jnp.float32).reshape(1, 1, Hkv, G, 1, 1), s.shape[:-1] + (1,))
    p = jax.nn.softmax(jnp.concatenate([s, sink], axis=-1), axis=-1)[..., :-1]
    o = jnp.einsum('bnhgqk,bnkhd->bnqhgd', p.astype(v.dtype), vv)
    return o.reshape(B, S, Hq * hd)


def _moba_branch(q, k, v):
    B, S, H, hd = q.shape
    BLK = MOBA_BLOCK
    QC = MOBA_Q_CHUNK
    nblk = -(-S // BLK)
    Sp = nblk * BLK
    padw = ((0, 0), (0, Sp - S), (0, 0), (0, 0))
    q, k, v = [jnp.pad(t, padw).transpose(0, 2, 1, 3) for t in (q, k, v)]
    kb = k.reshape(B, H, nblk, BLK, hd)
    vb = v.reshape(B, H, nblk, BLK, hd)
    kmean = kb.astype(jnp.float32).mean(axis=3)
    gate = jnp.einsum('bhsd,bhnd->bhsn', q.astype(jnp.float32), kmean)
    qblk = jnp.arange(Sp) // BLK
    past = jnp.arange(nblk)[None, :] < qblk[:, None]
    gate = jnp.where(past, gate, -jnp.inf)
    topk = min(MOBA_TOPK, nblk)
    _, idx = lax.top_k(gate, topk)
    valid = idx < qblk[:, None]
    nc = Sp // QC
    scale = hd ** -0.5

    def chunks(t):
        return jnp.moveaxis(t.reshape((B, H, nc, QC) + t.shape[3:]), 2, 0)

    bi = jnp.arange(B)[:, None, None, None]
    hi = jnp.arange(H)[None, :, None, None]

    def one_chunk(inp):
        qc, idxc, validc, c = inp
        kg = kb[bi, hi, idxc]
        vg = vb[bi, hi, idxc]
        s_sel = jnp.einsum('bhqd,bhqkjd->bhqkj', qc, kg).astype(jnp.float32) * scale
        s_sel = jnp.where(validc[..., None], s_sel, -jnp.inf).reshape(B, H, QC, topk * BLK)
        own = (c * QC) // BLK
        ko = lax.dynamic_index_in_dim(kb, own, axis=2, keepdims=False)
        vo = lax.dynamic_index_in_dim(vb, own, axis=2, keepdims=False)
        s_own = jnp.einsum('bhqd,bhjd->bhqj', qc, ko).astype(jnp.float32) * scale
        qpos = c * QC + jnp.arange(QC)
        kpos = own * BLK + jnp.arange(BLK)
        s_own = jnp.where(kpos[None, :] <= qpos[:, None], s_own, -jnp.inf)
        p = jax.nn.softmax(jnp.concatenate([s_sel, s_own], axis=-1), axis=-1).astype(v.dtype)
        p_sel = p[..., :topk * BLK].reshape(B, H, QC, topk, BLK)
        p_own = p[..., topk * BLK:]
        return (jnp.einsum('bhqkj,bhqkjd->bhqd', p_sel, vg)
                + jnp.einsum('bhqj,bhjd->bhqd', p_own, vo))

    o = lax.map(one_chunk, (chunks(q), chunks(idx), chunks(valid), jnp.arange(nc)))
    o = o.transpose(1, 0, 3, 2, 4).reshape(B, Sp, H * hd)
    return o[:, :S]


def _hybrid_layer(x, w_in, gla_w_up, gla_b, gla_norm_g, swa_sinks, b_merge, w_branch, w_o, ln_g, ln_b):
    B, S, D = x.shape
    proj = x @ w_in
    split_idx = np.cumsum(IN_SPLITS)[:-1].tolist()
    (aq, ak, av, alr, agate, bq, bk, bv, bgate, cq, ck, cv, cgate, mgate) = jnp.split(proj, split_idx, axis=-1)
    g = jax.nn.log_sigmoid((alr @ gla_w_up + gla_b).astype(jnp.float32)) / GLA_TAU
    ya = _gla_branch(aq.reshape(B, S, GLA_HEADS, GLA_DK), ak.reshape(B, S, GLA_HEADS, GLA_DK),
                     av.reshape(B, S, GLA_HEADS, GLA_DV), g.reshape(B, S, GLA_HEADS, GLA_DK), gla_norm_g)
    ya = ya * jax.nn.silu(agate)
    yb = _swa_branch(bq.reshape(B, S, SWA_Q_HEADS, SWA_HEAD_DIM), bk.reshape(B, S, SWA_KV_HEADS, SWA_HEAD_DIM),
                     bv.reshape(B, S, SWA_KV_HEADS, SWA_HEAD_DIM), swa_sinks)
    yb = yb * jax.nn.silu(bgate)
    yc = _moba_branch(cq.reshape(B, S, MOBA_HEADS, MOBA_HEAD_DIM), ck.reshape(B, S, MOBA_HEADS, MOBA_HEAD_DIM),
                      cv.reshape(B, S, MOBA_HEADS, MOBA_HEAD_DIM))
    yc = yc * jax.nn.silu(cgate)
    ys = jnp.stack([ya.astype(x.dtype), yb.astype(x.dtype), yc.astype(x.dtype)], axis=2)
    up = jnp.einsum('bsnw,nwd->bsnd', ys, w_branch)
    gates = jax.nn.sigmoid(mgate.reshape(B, S, N_BRANCH, D) + b_merge)
    merged = jnp.sum(gates * up, axis=2)
    out = merged @ w_o
    return _layer_norm(DEEPNORM_ALPHA * x + out, ln_g, ln_b)


def setup_inputs(seed: int = 0) -> dict:
    key = jax.random.key(seed)
    ks = jax.random.split(key, 12)
    L, D = DEPTH, D_MODEL
    col_scale = np.concatenate([np.full((n,), DEEPNORM_BETA if i in VALUE_SEGMENTS else 1.0, np.float32)
                                for i, n in enumerate(IN_SPLITS)])
    x = jax.random.normal(ks[0], (BATCH, SEQ, D), jnp.float32)
    w_in = jax.random.normal(ks[1], (L, D, D_IN), jnp.float32) * (D ** -0.5) * jnp.asarray(col_scale)
    gla_w_up = jax.random.normal(ks[2], (L, GLA_RANK, GLA_HEADS * GLA_DK), jnp.float32) * (GLA_RANK ** -0.5)
    gla_b = 0.1 * jax.random.normal(ks[3], (L, GLA_HEADS * GLA_DK), jnp.float32)
    gla_norm_g = 1.0 + 0.02 * jax.random.normal(ks[4], (L, GLA_DV), jnp.float32)
    swa_sinks = 0.5 * jax.random.normal(ks[5], (L, SWA_Q_HEADS), jnp.float32)
    b_merge = 0.02 * jax.random.normal(ks[6], (L, N_BRANCH, D), jnp.float32)
    w_branch = jax.random.normal(ks[7], (L, N_BRANCH, BRANCH_WIDTH, D), jnp.float32) * (BRANCH_WIDTH ** -0.5) * DEEPNORM_BETA
    w_o = jax.random.normal(ks[8], (L, D, D), jnp.float32) * (D ** -0.5) * DEEPNORM_BETA
    ln_g = 1.0 + 0.02 * jax.random.normal(ks[9], (L, D), jnp.float32)
    ln_b = 0.02 * jax.random.normal(ks[10], (L, D), jnp.float32)
    return {"x": x, "w_in": w_in, "gla_w_up": gla_w_up, "gla_b": gla_b, "gla_norm_g": gla_norm_g,
            "swa_sinks": swa_sinks, "b_merge": b_merge, "w_branch": w_branch, "w_o": w_o,
            "ln_g": ln_g, "ln_b": ln_b}


def reference(x, w_in, gla_w_up, gla_b, gla_norm_g, swa_sinks, b_merge, w_branch, w_o, ln_g, ln_b):
    for l in range(DEPTH):
        x = _hybrid_layer(x, w_in[l], gla_w_up[l], gla_b[l], gla_norm_g[l], swa_sinks[l], b_merge[l],
                          w_branch[l], w_o[l], ln_g[l], ln_b[l])
    return x
```

```python
import functools

import jax
import jax.numpy as jnp
import numpy as np
from jax import lax
from jax.experimental import pallas as pl
from jax.experimental.pallas import tpu as pltpu

F32 = jnp.float32
BF16 = jnp.bfloat16

D_MODEL = 2048
BRANCH_WIDTH = 1024
N_BRANCH = 3
GLA_HEADS, GLA_DK, GLA_DV, GLA_RANK, GLA_TAU = 4, 128, 256, 16, 16.0
SWA_Q_HEADS, SWA_KV_HEADS, SWA_HEAD_DIM, SWA_WINDOW = 16, 2, 64, 128
MOBA_HEADS, MOBA_HEAD_DIM, MOBA_BLOCK, MOBA_TOPK = 8, 128, 256, 3
LN_EPS = 1e-5
RMS_EPS = 1e-6

IN_SPLITS = (
    GLA_HEADS * GLA_DK, GLA_HEADS * GLA_DK, GLA_HEADS * GLA_DV, GLA_RANK, BRANCH_WIDTH,
    SWA_Q_HEADS * SWA_HEAD_DIM, SWA_KV_HEADS * SWA_HEAD_DIM, SWA_KV_HEADS * SWA_HEAD_DIM, BRANCH_WIDTH,
    MOBA_HEADS * MOBA_HEAD_DIM, MOBA_HEADS * MOBA_HEAD_DIM, MOBA_HEADS * MOBA_HEAD_DIM, BRANCH_WIDTH,
    N_BRANCH * D_MODEL,
)
(SEG_AQ, SEG_AK, SEG_AV, SEG_ALR, SEG_AGATE, SEG_BQ, SEG_BK, SEG_BV, SEG_BGATE,
 SEG_CQ, SEG_CK, SEG_CV, SEG_CGATE, SEG_MGATE) = range(14)

LANES = 128
V7X_VMEM_BYTES = 64 * 1024 * 1024

PROJ_ORDER = (SEG_MGATE, SEG_AV, SEG_AGATE, SEG_BQ, SEG_BGATE, SEG_CQ, SEG_CK, SEG_CV, SEG_CGATE,
              SEG_AQ, SEG_AK, SEG_BK, SEG_BV, SEG_ALR)
PROJ_TILE_N = 512


def _proj_layout():
    src = np.concatenate([[0], np.cumsum(IN_SPLITS)])
    offs, pieces, cur = {}, [], 0
    for seg in PROJ_ORDER:
        width = IN_SPLITS[seg]
        padded = -(-width // LANES) * LANES
        offs[seg] = cur
        pieces.append((int(src[seg]), width, padded - width))
        cur += padded
    total = -(-cur // PROJ_TILE_N) * PROJ_TILE_N
    return offs, pieces, cur, total


PROJ_OFF, PROJ_PIECES, PROJ_USED, PROJ_COLS = _proj_layout()


def _permute_w_in(w_in):
    parts = []
    for start, width, pad in PROJ_PIECES:
        piece = w_in[:, :, start:start + width]
        if pad:
            piece = jnp.pad(piece, ((0, 0), (0, 0), (0, pad)))
        parts.append(piece)
    if PROJ_COLS > PROJ_USED:
        parts.append(jnp.zeros(w_in.shape[:2] + (PROJ_COLS - PROJ_USED,), w_in.dtype))
    return jnp.concatenate(parts, axis=-1).astype(BF16)


def _silu(x):
    return x * (1.0 / (1.0 + jnp.exp(-x)))


def _sigmoid(x):
    return 1.0 / (1.0 + jnp.exp(-x))


def _dot_nt(a, b):
    return lax.dot_general(a, b, (((1,), (1,)), ((), ())), preferred_element_type=F32)


PROJ_TILE_M = 1024


def _proj_kernel(x_ref, w_ref, o_ref):
    o_ref[...] = jnp.dot(x_ref[...], w_ref[...], preferred_element_type=F32).astype(o_ref.dtype)


def _proj(xb, w):
    m, d = xb.shape
    n = w.shape[1]
    tm, tn = min(PROJ_TILE_M, m), PROJ_TILE_N
    vmem = 2 * (tm * d * 2 + d * tn * 2 + tm * tn * 2) + (8 << 20)
    return pl.pallas_call(
        _proj_kernel,
        grid=(m // tm, n // tn),
        in_specs=[pl.BlockSpec((tm, d), lambda i, j: (i, 0)),
                  pl.BlockSpec((d, tn), lambda i, j: (0, j))],
        out_specs=pl.BlockSpec((tm, tn), lambda i, j: (i, j)),
        out_shape=jax.ShapeDtypeStruct((m, n), BF16),
        compiler_params=pltpu.CompilerParams(
            dimension_semantics=("parallel", "arbitrary"), vmem_limit_bytes=vmem),
        name="in_proj",
    )(xb, w)


GLA_CHUNK_ROWS = 128
GLA_SUB = 16


def _gla_kernel(q_ref, k_ref, v_ref, gate_ref, alr_ref, wup_ref, bias_ref, ng_ref, o_ref,
                s_ref, b_ref, qs_ref, kf_ref, vf_ref, acc_ref):
    C = GLA_CHUNK_ROWS

    @pl.when(pl.program_id(2) == 0)
    def _():
        s_ref[...] = jnp.zeros_like(s_ref)

    z = jnp.dot(alr_ref[...], wup_ref[...], preferred_element_type=F32) + bias_ref[...]
    g = -(jnp.maximum(-z, 0.0) + jnp.log1p(jnp.exp(-jnp.abs(z)))) * (1.0 / GLA_TAU)

    row = lax.broadcasted_iota(jnp.int32, (C, C), 0)
    col = lax.broadcasted_iota(jnp.int32, (C, C), 1)
    tril = jnp.where(col <= row, 1.0, 0.0).astype(BF16)
    g1 = g.astype(BF16)
    r1 = g - g1.astype(F32)
    g2 = r1.astype(BF16)
    g3 = (r1 - g2.astype(F32)).astype(BF16)
    b = (jnp.dot(tril, g1, preferred_element_type=F32)
         + jnp.dot(tril, g2, preferred_element_type=F32)
         + jnp.dot(tril, g3, preferred_element_type=F32))

    qs = q_ref[...].astype(F32) * (GLA_DK ** -0.5)
    kf = k_ref[...].astype(F32)
    vb = v_ref[...]
    b_ref[...] = b
    qs_ref[...] = qs
    kf_ref[...] = kf
    vf_ref[...] = vb.astype(F32)

    s_old = s_ref[...]
    acc_ref[...] = jnp.dot((qs * jnp.exp(b)).astype(BF16), s_old.astype(BF16),
                           preferred_element_type=F32)

    key_idx = lax.broadcasted_iota(jnp.int32, (GLA_SUB, C), 1)
    sub_row = lax.broadcasted_iota(jnp.int32, (GLA_SUB, GLA_DK), 0)

    def sub_block(i, carry):
        base = pl.multiple_of(i * GLA_SUB, GLA_SUB)
        b_start = b_ref[pl.ds(jnp.maximum(base - 1, 0), 1), :]
        b_i = b_ref[pl.ds(base, GLA_SUB), :]
        qs_i = qs_ref[pl.ds(base, GLA_SUB), :]
        kx = kf * jnp.exp(jnp.minimum(b_start - b, 0.0))
        qx = qs_i * jnp.exp(jnp.minimum(b_i - b_start, 0.0))
        s = _dot_nt(qx.astype(BF16), kx.astype(BF16))
        s = jnp.where(key_idx < base, s, 0.0)
        o_i = jnp.dot(s.astype(BF16), vb, preferred_element_type=F32)
        for j in range(GLA_SUB):
            b_j = b_ref[pl.ds(base + j, 1), :]
            k_j = kf_ref[pl.ds(base + j, 1), :]
            v_j = vf_ref[pl.ds(base + j, 1), :]
            t = jnp.exp(jnp.minimum(b_i - b_j, 0.0)) * (qs_i * k_j)
            t = jnp.where(sub_row >= j, t, 0.0)
            o_i = o_i + jnp.sum(t, axis=-1, keepdims=True) * v_j
        acc_ref[pl.ds(base, GLA_SUB), :] += o_i
        return carry

    lax.fori_loop(0, C // GLA_SUB, sub_block, 0)

    b_end = b[C - 1:C, :]
    k_end = kf * jnp.exp(b_end - b)
    decay_col = jnp.transpose(jnp.broadcast_to(jnp.exp(b_end), (GLA_DK, GLA_DK)))
    decay = jnp.concatenate([decay_col] * (GLA_DV // GLA_DK), axis=1)
    s_ref[...] = s_old * decay + jnp.dot(jnp.transpose(k_end).astype(BF16), vb,
                                         preferred_element_type=F32)

    o = acc_ref[...]
    o = o * lax.rsqrt(jnp.mean(o * o, axis=-1, keepdims=True) + RMS_EPS) * ng_ref[...]
    o_ref[...] = (o * _silu(gate_ref[...].astype(F32))).astype(o_ref.dtype)


def _gla(p, wup, bias, ng, batch, seq):
    C = GLA_CHUNK_ROWS
    nc = seq // C
    rows = lambda b, h, c: b * nc + c
    dkb, dvb = GLA_DK, GLA_DV
    q0, k0 = PROJ_OFF[SEG_AQ] // dkb, PROJ_OFF[SEG_AK] // dkb
    v0, g0 = PROJ_OFF[SEG_AV] // dvb, PROJ_OFF[SEG_AGATE] // dvb
    lr0 = PROJ_OFF[SEG_ALR] // LANES
    return pl.pallas_call(
        _gla_kernel,
        grid=(batch, GLA_HEADS, nc),
        in_specs=[
            pl.BlockSpec((C, dkb), lambda b, h, c: (rows(b, h, c), q0 + h)),
            pl.BlockSpec((C, dkb), lambda b, h, c: (rows(b, h, c), k0 + h)),
            pl.BlockSpec((C, dvb), lambda b, h, c: (rows(b, h, c), v0 + h)),
            pl.BlockSpec((C, dvb), lambda b, h, c: (rows(b, h, c), g0 + h)),
            pl.BlockSpec((C, LANES), lambda b, h, c: (rows(b, h, c), lr0)),
            pl.BlockSpec((LANES, dkb), lambda b, h, c: (0, h)),
            pl.BlockSpec((1, dkb), lambda b, h, c: (0, h)),
            pl.BlockSpec((1, dvb), lambda b, h, c: (0, 0)),
        ],
        out_specs=pl.BlockSpec((C, dvb), lambda b, h, c: (rows(b, h, c), h)),
        out_shape=jax.ShapeDtypeStruct((batch * seq, GLA_HEADS * GLA_DV), BF16),
        scratch_shapes=[
            pltpu.VMEM((GLA_DK, GLA_DV), F32),
            pltpu.VMEM((C, GLA_DK), F32),
            pltpu.VMEM((C, GLA_DK), F32),
            pltpu.VMEM((C, GLA_DK), F32),
            pltpu.VMEM((C, GLA_DV), F32),
            pltpu.VMEM((C, GLA_DV), F32),
        ],
        compiler_params=pltpu.CompilerParams(
            dimension_semantics=("parallel", "parallel", "arbitrary")),
        name="gla",
    )(p, p, p, p, p, wup, bias, ng)


def _swa_kernel(sink_ref, q_ref, kp_ref, kc_ref, vp_ref, vc_ref, gate_ref, o_ref):
    W, hd = SWA_WINDOW, SWA_HEAD_DIM
    n = pl.program_id(1)
    kk = jnp.concatenate([kp_ref[...], kc_ref[...]], axis=0)
    vv = jnp.concatenate([vp_ref[...], vc_ref[...]], axis=0)
    lane = lax.broadcasted_iota(jnp.int32, (2 * W, LANES), 1)
    r = lax.broadcasted_iota(jnp.int32, (W, 2 * W), 0)
    c = lax.broadcasted_iota(jnp.int32, (W, 2 * W), 1)
    allowed = (c > r) & (c <= r + W) & ((c >= W) | (n > 0))
    swap = lambda t: jnp.concatenate([t[:, hd:], t[:, :hd]], axis=1)
    group = SWA_Q_HEADS // SWA_KV_HEADS
    for hk in range(SWA_KV_HEADS):
        own = (lane >= hk * hd) & (lane < (hk + 1) * hd)
        k_own = jnp.where(own, kk, jnp.zeros_like(kk))
        v_own = jnp.where(own, vv, jnp.zeros_like(vv))
        k_half = (k_own, swap(k_own)) if hk == 0 else (swap(k_own), k_own)
        v_half = (v_own, swap(v_own)) if hk == 0 else (swap(v_own), v_own)
        for pair in range(group // 2):
            head0 = hk * group + 2 * pair
            col0 = head0 * hd
            qp = q_ref[:, col0:col0 + LANES]
            o_pair = jnp.zeros((W, LANES), F32)
            for half in range(2):
                sink = sink_ref[head0 + half]
                s = _dot_nt(qp, k_half[half]) * (hd ** -0.5)
                s = jnp.where(allowed, s, -1e30)
                m = jnp.maximum(jnp.max(s, axis=-1, keepdims=True), sink)
                p = jnp.exp(s - m)
                den = jnp.sum(p, axis=-1, keepdims=True) + jnp.exp(sink - m)
                p = p * (1.0 / den)
                o_pair = o_pair + jnp.dot(p.astype(BF16), v_half[half], preferred_element_type=F32)
            gate = gate_ref[:, col0:col0 + LANES].astype(F32)
            o_ref[:, col0:col0 + LANES] = (o_pair * _silu(gate)).astype(o_ref.dtype)


def _swa(p, sinks, batch, seq):
    W = SWA_WINDOW
    nb = seq // W
    width = SWA_Q_HEADS * SWA_HEAD_DIM
    q0, g0 = PROJ_OFF[SEG_BQ] // width, PROJ_OFF[SEG_BGATE] // width
    k0, v0 = PROJ_OFF[SEG_BK] // LANES, PROJ_OFF[SEG_BV] // LANES
    cur = lambda b, n: b * nb + n
    prev = lambda b, n: b * nb + jnp.maximum(n - 1, 0)
    return pl.pallas_call(
        _swa_kernel,
        grid=(batch, nb),
        in_specs=[
            pl.BlockSpec(memory_space=pltpu.SMEM),
            pl.BlockSpec((W, width), lambda b, n: (cur(b, n), q0)),
            pl.BlockSpec((W, LANES), lambda b, n: (prev(b, n), k0)),
            pl.BlockSpec((W, LANES), lambda b, n: (cur(b, n), k0)),
            pl.BlockSpec((W, LANES), lambda b, n: (prev(b, n), v0)),
            pl.BlockSpec((W, LANES), lambda b, n: (cur(b, n), v0)),
            pl.BlockSpec((W, width), lambda b, n: (cur(b, n), g0)),
        ],
        out_specs=pl.BlockSpec((W, width), lambda b, n: (cur(b, n), 0)),
        out_shape=jax.ShapeDtypeStruct((batch * seq, width), BF16),
        compiler_params=pltpu.CompilerParams(dimension_semantics=("parallel", "parallel")),
        name="swa",
    )(sinks, p, p, p, p, p, p)


MASK_VALUE = -1e30


def _moba_kernel(q_ref, k_ref, v_ref, gate_ref, o_ref, kmean_ref):
    BLK, hd = MOBA_BLOCK, MOBA_HEAD_DIM
    i = pl.program_id(2)
    nblk = k_ref.shape[0] // BLK

    @pl.when(i == 0)
    def _():
        kf = k_ref[...].astype(F32).reshape(nblk, BLK, hd)
        kmean_ref[...] = jnp.zeros_like(kmean_ref)
        kmean_ref[0:nblk, :] = jnp.sum(kf, axis=1) * (1.0 / BLK)

    q = q_ref[...]
    gate = lax.dot_general(q.astype(F32), kmean_ref[...], (((1,), (1,)), ((), ())),
                           preferred_element_type=F32, precision=lax.Precision.HIGHEST)
    lane = lax.broadcasted_iota(jnp.int32, gate.shape, 1)
    neg_inf = jnp.float32(-jnp.inf)
    g = jnp.where(lane < i, gate, neg_inf)
    sel = jnp.zeros(gate.shape, jnp.bool_)
    for _ in range(MOBA_TOPK):
        m = jnp.max(g, axis=-1, keepdims=True)
        idx = jnp.min(jnp.where(g == m, lane, LANES), axis=-1, keepdims=True)
        pick = (lane == idx) & (m > neg_inf)
        sel = sel | pick
        g = jnp.where(pick, neg_inf, g)
    sel_bias = jnp.where(sel, 0.0, MASK_VALUE)

    scale = hd ** -0.5
    own = pl.multiple_of(i * BLK, BLK)
    r = lax.broadcasted_iota(jnp.int32, (BLK, BLK), 0)
    c = lax.broadcasted_iota(jnp.int32, (BLK, BLK), 1)
    s = _dot_nt(q, k_ref[pl.ds(own, BLK), :]) * scale
    s = jnp.where(c <= r, s, MASK_VALUE)
    m0 = jnp.max(s, axis=-1, keepdims=True)
    p = jnp.exp(s - m0)
    l0 = jnp.sum(p, axis=-1, keepdims=True)
    acc0 = jnp.dot(p.astype(BF16), v_ref[pl.ds(own, BLK), :], preferred_element_type=F32)

    def past_block(n, carry):
        m_prev, l_prev, acc = carry
        start = pl.multiple_of(n * BLK, BLK)
        bias = jnp.max(jnp.where(lane == n, sel_bias, MASK_VALUE), axis=-1, keepdims=True)
        s = _dot_nt(q, k_ref[pl.ds(start, BLK), :]) * scale + bias
        m_new = jnp.maximum(m_prev, jnp.max(s, axis=-1, keepdims=True))
        alpha = jnp.exp(m_prev - m_new)
        p = jnp.exp(s - m_new)
        l_new = alpha * l_prev + jnp.sum(p, axis=-1, keepdims=True)
        acc = alpha * acc + jnp.dot(p.astype(BF16), v_ref[pl.ds(start, BLK), :],
                                    preferred_element_type=F32)
        return m_new, l_new, acc

    _, l_fin, acc = lax.fori_loop(0, i, past_block, (m0, l0, acc0))
    o = acc * (1.0 / l_fin)
    o_ref[...] = (o * _silu(gate_ref[...].astype(F32))).astype(o_ref.dtype)


def _moba(p, batch, seq):
    BLK, hd = MOBA_BLOCK, MOBA_HEAD_DIM
    nblk = seq // BLK
    assert seq % BLK == 0 and nblk <= LANES
    q0, k0 = PROJ_OFF[SEG_CQ] // hd, PROJ_OFF[SEG_CK] // hd
    v0, g0 = PROJ_OFF[SEG_CV] // hd, PROJ_OFF[SEG_CGATE] // hd
    return pl.pallas_call(
        _moba_kernel,
        grid=(batch, MOBA_HEADS, nblk),
        in_specs=[
            pl.BlockSpec((BLK, hd), lambda b, h, i: (b * nblk + i, q0 + h)),
            pl.BlockSpec((seq, hd), lambda b, h, i: (b, k0 + h)),
            pl.BlockSpec((seq, hd), lambda b, h, i: (b, v0 + h)),
            pl.BlockSpec((BLK, hd), lambda b, h, i: (b * nblk + i, g0 + h)),
        ],
        out_specs=pl.BlockSpec((BLK, hd), lambda b, h, i: (b * nblk + i, h)),
        out_shape=jax.ShapeDtypeStruct((batch * seq, MOBA_HEADS * hd), BF16),
        scratch_shapes=[pltpu.VMEM((LANES, hd), F32)],
        compiler_params=pltpu.CompilerParams(
            dimension_semantics=("parallel", "parallel", "arbitrary")),
        name="moba",
    )(p, p, p, p)


MERGE_TILE_M = 256


def _merge_kernel(alpha, ya_ref, yb_ref, yc_ref, mg_ref, x_ref, wb_ref, bm_ref, wo_ref, lg_ref, lb_ref,
                  y_ref, yb16_ref):
    D = D_MODEL
    merged = None
    for n, y_n in enumerate((ya_ref, yb_ref, yc_ref)):
        up = jnp.dot(y_n[...], wb_ref[n], preferred_element_type=F32)
        gate = _sigmoid(mg_ref[:, n * D:(n + 1) * D].astype(F32) + bm_ref[n:n + 1, :])
        merged = gate * up if merged is None else merged + gate * up
    out = jnp.dot(merged.astype(BF16), wo_ref[...], preferred_element_type=F32)
    h = alpha * x_ref[...] + out
    mu = jnp.mean(h, axis=-1, keepdims=True)
    hc = h - mu
    var = jnp.mean(hc * hc, axis=-1, keepdims=True)
    y = hc * lax.rsqrt(var + LN_EPS) * lg_ref[...] + lb_ref[...]
    y_ref[...] = y
    yb16_ref[...] = y.astype(BF16)


def _merge(ya, yb, yc, p, x, wb, bm, wo, lg, lb, alpha):
    m = x.shape[0]
    D, Wd = D_MODEL, BRANCH_WIDTH
    tm = min(MERGE_TILE_M, m)
    once = pl.Buffered(1)
    resident = (N_BRANCH * Wd * D + D * D) * 2
    streamed = 2 * (3 * tm * Wd * 2 + tm * 3 * D * 2 + tm * D * 4 + tm * D * 4 + tm * D * 2)
    vmem = resident + streamed + (12 << 20)
    row = lambda i: (i, 0)
    fixed2 = lambda i: (0, 0)
    return pl.pallas_call(
        functools.partial(_merge_kernel, alpha),
        grid=(m // tm,),
        in_specs=[
            pl.BlockSpec((tm, Wd), row), pl.BlockSpec((tm, Wd), row), pl.BlockSpec((tm, Wd), row),
            pl.BlockSpec((tm, N_BRANCH * D), lambda i: (i, PROJ_OFF[SEG_MGATE] // (N_BRANCH * D))),
            pl.BlockSpec((tm, D), row),
            pl.BlockSpec((N_BRANCH, Wd, D), lambda i: (0, 0, 0), pipeline_mode=once),
            pl.BlockSpec((N_BRANCH, D), fixed2),
            pl.BlockSpec((D, D), fixed2, pipeline_mode=once),
            pl.BlockSpec((1, D), fixed2), pl.BlockSpec((1, D), fixed2),
        ],
        out_specs=[pl.BlockSpec((tm, D), row), pl.BlockSpec((tm, D), row)],
        out_shape=[jax.ShapeDtypeStruct((m, D), F32), jax.ShapeDtypeStruct((m, D), BF16)],
        compiler_params=pltpu.CompilerParams(
            dimension_semantics=("parallel",), vmem_limit_bytes=min(vmem, V7X_VMEM_BYTES - (4 << 20))),
        name="merge_out_ln",
    )(ya, yb, yc, p, x, wb, bm, wo, lg, lb)


def kernel(x, w_in, gla_w_up, gla_b, gla_norm_g, swa_sinks, b_merge, w_branch, w_o, ln_g, ln_b):
    batch, seq, d = x.shape
    depth = w_in.shape[0]
    alpha = (2 * depth) ** 0.25
    assert d == D_MODEL and seq % MOBA_BLOCK == 0 and seq % GLA_CHUNK_ROWS == 0

    w_in_p = _permute_w_in(w_in)
    wup = jnp.pad(gla_w_up, ((0, 0), (0, LANES - GLA_RANK), (0, 0))).astype(BF16)
    wb = w_branch.astype(BF16)
    wo = w_o.astype(BF16)

    xf = x.reshape(batch * seq, d)
    xb = xf.astype(BF16)
    for l in range(depth):
        p = _proj(xb, w_in_p[l])
        ya = _gla(p, wup[l], gla_b[l][None, :], gla_norm_g[l][None, :], batch, seq)
        yb = _swa(p, swa_sinks[l], batch, seq)
        yc = _moba(p, batch, seq)
        xf, xb = _merge(ya, yb, yc, p, xf, wb[l], b_merge[l], wo[l], ln_g[l][None, :], ln_b[l][None, :],
                        alpha)
    return xf.reshape(batch, seq, d)
```

```python
import functools

import jax
import jax.numpy as jnp
import numpy as np
from jax import lax
from jax.experimental import pallas as pl
from jax.experimental.pallas import tpu as pltpu

F32 = jnp.float32
BF16 = jnp.bfloat16

D_MODEL = 2048
BRANCH_WIDTH = 1024
N_BRANCH = 3
GLA_HEADS, GLA_DK, GLA_DV, GLA_RANK, GLA_TAU = 4, 128, 256, 16, 16.0
SWA_Q_HEADS, SWA_KV_HEADS, SWA_HEAD_DIM, SWA_WINDOW = 16, 2, 64, 128
MOBA_HEADS, MOBA_HEAD_DIM, MOBA_BLOCK, MOBA_TOPK = 8, 128, 256, 3
LN_EPS = 1e-5
RMS_EPS = 1e-6

IN_SPLITS = (
    GLA_HEADS * GLA_DK, GLA_HEADS * GLA_DK, GLA_HEADS * GLA_DV, GLA_RANK, BRANCH_WIDTH,
    SWA_Q_HEADS * SWA_HEAD_DIM, SWA_KV_HEADS * SWA_HEAD_DIM, SWA_KV_HEADS * SWA_HEAD_DIM, BRANCH_WIDTH,
    MOBA_HEADS * MOBA_HEAD_DIM, MOBA_HEADS * MOBA_HEAD_DIM, MOBA_HEADS * MOBA_HEAD_DIM, BRANCH_WIDTH,
    N_BRANCH * D_MODEL,
)
(SEG_AQ, SEG_AK, SEG_AV, SEG_ALR, SEG_AGATE, SEG_BQ, SEG_BK, SEG_BV, SEG_BGATE,
 SEG_CQ, SEG_CK, SEG_CV, SEG_CGATE, SEG_MGATE) = range(14)

LANES = 128
V7X_VMEM_BYTES = 64 * 1024 * 1024

PROJ_ORDER = (SEG_MGATE, SEG_AV, SEG_AGATE, SEG_BQ, SEG_BGATE, SEG_CQ, SEG_CK, SEG_CV, SEG_CGATE,
              SEG_AQ, SEG_AK, SEG_BK, SEG_BV, SEG_ALR)
PROJ_TILE_N = 512


def _proj_layout():
    src = np.concatenate([[0], np.cumsum(IN_SPLITS)])
    offs, pieces, cur = {}, [], 0
    for seg in PROJ_ORDER:
        width = IN_SPLITS[seg]
        padded = -(-width // LANES) * LANES
        offs[seg] = cur
        pieces.append((int(src[seg]), width, padded - width))
        cur += padded
    total = -(-cur // PROJ_TILE_N) * PROJ_TILE_N
    return offs, pieces, cur, total


PROJ_OFF, PROJ_PIECES, PROJ_USED, PROJ_COLS = _proj_layout()


PERMUTE_TILE_ROWS = 128
PERMUTE_CHUNK_COLS = 512


def _permute_kernel(w_ref, o_ref):
    d_in = w_ref.shape[1]
    dst = 0
    for src, width, pad in PROJ_PIECES:
        for c0 in range(0, width, PERMUTE_CHUNK_COLS):
            w = min(PERMUTE_CHUNK_COLS, width - c0)
            s0 = src + c0
            a0 = s0 // LANES * LANES
            a1 = min(-(-(s0 + w) // LANES) * LANES, d_in)
            val = w_ref[:, a0:a1]
            o_ref[:, dst + c0:dst + c0 + w] = val[:, s0 - a0:s0 - a0 + w].astype(o_ref.dtype)
        if pad:
            o_ref[:, dst + width:dst + width + pad] = jnp.zeros((o_ref.shape[0], pad), o_ref.dtype)
        dst += width + pad
    if PROJ_COLS > dst:
        o_ref[:, dst:] = jnp.zeros((o_ref.shape[0], PROJ_COLS - dst), o_ref.dtype)


def _permute_w_in(w_in):
    depth, d, d_in = w_in.shape
    tr = PERMUTE_TILE_ROWS
    vmem = 2 * (tr * d_in * 4 + tr * PROJ_COLS * 2) + (8 << 20)
    return pl.pallas_call(
        _permute_kernel,
        grid=(depth, d // tr),
        in_specs=[pl.BlockSpec((None, tr, d_in), lambda l, i: (l, i, 0))],
        out_specs=pl.BlockSpec((None, tr, PROJ_COLS), lambda l, i: (l, i, 0)),
        out_shape=jax.ShapeDtypeStruct((depth, d, PROJ_COLS), BF16),
        compiler_params=pltpu.CompilerParams(
            dimension_semantics=("parallel", "parallel"), vmem_limit_bytes=vmem),
        name="permute_w_in",
    )(w_in)


def _silu(x):
    return x * (1.0 / (1.0 + jnp.exp(-x)))


def _sigmoid(x):
    return 1.0 / (1.0 + jnp.exp(-x))


def _dot_nt(a, b):
    return lax.dot_general(a, b, (((1,), (1,)), ((), ())), preferred_element_type=F32)


PROJ_TILE_M = 1024


def _proj_kernel(x_ref, w_ref, o_ref):
    o_ref[...] = jnp.dot(x_ref[...], w_ref[...], preferred_element_type=F32).astype(o_ref.dtype)


def _proj(xb, w):
    m, d = xb.shape
    n = w.shape[1]
    tm, tn = min(PROJ_TILE_M, m), PROJ_TILE_N
    vmem = 2 * (tm * d * 2 + d * tn * 2 + tm * tn * 2) + (8 << 20)
    return pl.pallas_call(
        _proj_kernel,
        grid=(m // tm, n // tn),
        in_specs=[pl.BlockSpec((tm, d), lambda i, j: (i, 0)),
                  pl.BlockSpec((d, tn), lambda i, j: (0, j))],
        out_specs=pl.BlockSpec((tm, tn), lambda i, j: (i, j)),
        out_shape=jax.ShapeDtypeStruct((m, n), BF16),
        compiler_params=pltpu.CompilerParams(
            dimension_semantics=("parallel", "arbitrary"), vmem_limit_bytes=vmem),
        name="in_proj",
    )(xb, w)


GLA_CHUNK_ROWS = 128
GLA_SUB = 16
GLA_FAST_MAX_DECAY = 30.0
GLA_FAST_MIN_Q = 1e-20
GLA_FAST_MAX_QK = 1e20


def _gla_kernel(q_ref, k_ref, v_ref, gate_ref, alr_ref, wup_ref, bias_ref, ng_ref, o_ref,
                s_ref, b_ref, qs_ref, qx_ref, kf_ref, vf_ref, acc_ref):
    C, H, dk, dv = GLA_CHUNK_ROWS, GLA_HEADS, GLA_DK, GLA_DV

    @pl.when(pl.program_id(1) == 0)
    def _():
        s_ref[...] = jnp.zeros_like(s_ref)

    z = jnp.dot(alr_ref[...], wup_ref[...], preferred_element_type=F32) + bias_ref[...]
    g = -(jnp.maximum(-z, 0.0) + jnp.log1p(jnp.exp(-jnp.abs(z)))) * (1.0 / GLA_TAU)

    row = lax.broadcasted_iota(jnp.int32, (C, C), 0)
    col = lax.broadcasted_iota(jnp.int32, (C, C), 1)
    tril = jnp.where(col <= row, 1.0, 0.0).astype(BF16)
    g1 = g.astype(BF16)
    r1 = g - g1.astype(F32)
    g2 = r1.astype(BF16)
    g3 = (r1 - g2.astype(F32)).astype(BF16)
    b_all = (jnp.dot(tril, g1, preferred_element_type=F32)
             + jnp.dot(tril, g2, preferred_element_type=F32)
             + jnp.dot(tril, g3, preferred_element_type=F32))

    q_max = jnp.float32(0.0)
    k_max = jnp.float32(0.0)
    for h in range(H):
        b = b_all[:, h * dk:(h + 1) * dk]
        qs = q_ref[:, h * dk:(h + 1) * dk].astype(F32) * (dk ** -0.5)
        kf = k_ref[:, h * dk:(h + 1) * dk].astype(F32)
        qx = (qs * jnp.exp(b)).astype(BF16)
        b_ref[h] = b
        qs_ref[h] = qs
        kf_ref[h] = kf
        qx_ref[h] = qx
        q_max = jnp.maximum(q_max, jnp.max(jnp.abs(qs)))
        k_max = jnp.maximum(k_max, jnp.max(jnp.abs(kf)))
        acc_ref[h] = jnp.dot(qx, s_ref[h].astype(BF16), preferred_element_type=F32)

    total_decay = jnp.max(-b_all[C - 1:C, :])
    in_range = ((total_decay <= GLA_FAST_MAX_DECAY) & (q_max >= GLA_FAST_MIN_Q)
                & (q_max <= GLA_FAST_MAX_QK) & (k_max <= GLA_FAST_MAX_QK))

    @pl.when(in_range)
    def _():
        for h in range(H):
            k_inv = (kf_ref[h] * jnp.exp(-b_ref[h])).astype(BF16)
            a = jnp.where(col <= row, _dot_nt(qx_ref[h], k_inv), 0.0)
            acc_ref[h] += jnp.dot(a.astype(BF16), v_ref[:, h * dv:(h + 1) * dv],
                                  preferred_element_type=F32)

    @pl.when(jnp.logical_not(in_range))
    def _():
        _gla_sub_blocks(v_ref, b_ref, qs_ref, kf_ref, vf_ref, acc_ref)

    for h in range(H):
        b = b_ref[h]
        b_end = b[C - 1:C, :]
        k_end = kf_ref[h] * jnp.exp(b_end - b)
        decay_col = jnp.transpose(jnp.broadcast_to(jnp.exp(b_end), (dk, dk)))
        decay = jnp.concatenate([decay_col] * (dv // dk), axis=1)
        s_ref[h] = s_ref[h] * decay + jnp.dot(jnp.transpose(k_end).astype(BF16),
                                              v_ref[:, h * dv:(h + 1) * dv],
                                              preferred_element_type=F32)
        o = acc_ref[h]
        o = o * lax.rsqrt(jnp.mean(o * o, axis=-1, keepdims=True) + RMS_EPS) * ng_ref[...]
        gate = gate_ref[:, h * dv:(h + 1) * dv].astype(F32)
        o_ref[:, h * dv:(h + 1) * dv] = (o * _silu(gate)).astype(o_ref.dtype)


def _gla_sub_blocks(v_ref, b_ref, qs_ref, kf_ref, vf_ref, acc_ref):
    C, H, dk, dv = GLA_CHUNK_ROWS, GLA_HEADS, GLA_DK, GLA_DV
    for h in range(H):
        vf_ref[h] = v_ref[:, h * dv:(h + 1) * dv].astype(F32)
    key_idx = lax.broadcasted_iota(jnp.int32, (GLA_SUB, C), 1)
    sub_row = lax.broadcasted_iota(jnp.int32, (GLA_SUB, dk), 0)

    def sub_block(i, carry):
        base = pl.multiple_of(i * GLA_SUB, GLA_SUB)
        prev = jnp.maximum(base - 1, 0)
        for h in range(H):
            b_start = b_ref[h, pl.ds(prev, 1), :]
            b_i = b_ref[h, pl.ds(base, GLA_SUB), :]
            qs_i = qs_ref[h, pl.ds(base, GLA_SUB), :]
            kx = kf_ref[h] * jnp.exp(jnp.minimum(b_start - b_ref[h], 0.0))
            qx = qs_i * jnp.exp(jnp.minimum(b_i - b_start, 0.0))
            s = _dot_nt(qx.astype(BF16), kx.astype(BF16))
            s = jnp.where(key_idx < base, s, 0.0)
            o_past = jnp.dot(s.astype(BF16), v_ref[:, h * dv:(h + 1) * dv], preferred_element_type=F32)
            terms = []
            for j in range(GLA_SUB):
                b_j = b_ref[h, pl.ds(base + j, 1), :]
                k_j = kf_ref[h, pl.ds(base + j, 1), :]
                v_j = vf_ref[h, pl.ds(base + j, 1), :]
                t = jnp.exp(jnp.minimum(b_i - b_j, 0.0)) * (qs_i * k_j)
                t = jnp.where(sub_row >= j, t, 0.0)
                terms.append(jnp.sum(t, axis=-1, keepdims=True) * v_j)
            while len(terms) > 1:
                terms = [a + b for a, b in zip(terms[0::2], terms[1::2])]
            acc_ref[h, pl.ds(base, GLA_SUB), :] += terms[0] + o_past
        return carry

    lax.fori_loop(0, C // GLA_SUB, sub_block, 0)


def _gla(p, wup, bias, ng, batch, seq):
    C, H, dk, dv = GLA_CHUNK_ROWS, GLA_HEADS, GLA_DK, GLA_DV
    nc = seq // C
    rows = lambda b, c: b * nc + c
    q0, k0 = PROJ_OFF[SEG_AQ] // (H * dk), PROJ_OFF[SEG_AK] // (H * dk)
    v0, g0 = PROJ_OFF[SEG_AV] // (H * dv), PROJ_OFF[SEG_AGATE] // (H * dv)
    lr0 = PROJ_OFF[SEG_ALR] // LANES
    return pl.pallas_call(
        _gla_kernel,
        grid=(batch, nc),
        in_specs=[
            pl.BlockSpec((C, H * dk), lambda b, c: (rows(b, c), q0)),
            pl.BlockSpec((C, H * dk), lambda b, c: (rows(b, c), k0)),
            pl.BlockSpec((C, H * dv), lambda b, c: (rows(b, c), v0)),
            pl.BlockSpec((C, H * dv), lambda b, c: (rows(b, c), g0)),
            pl.BlockSpec((C, LANES), lambda b, c: (rows(b, c), lr0)),
            pl.BlockSpec((LANES, H * dk), lambda b, c: (0, 0)),
            pl.BlockSpec((1, H * dk), lambda b, c: (0, 0)),
            pl.BlockSpec((1, dv), lambda b, c: (0, 0)),
        ],
        out_specs=pl.BlockSpec((C, H * dv), lambda b, c: (rows(b, c), 0)),
        out_shape=jax.ShapeDtypeStruct((batch * seq, H * dv), BF16),
        scratch_shapes=[
            pltpu.VMEM((H, dk, dv), F32),
            pltpu.VMEM((H, C, dk), F32),
            pltpu.VMEM((H, C, dk), F32),
            pltpu.VMEM((H, C, dk), BF16),
            pltpu.VMEM((H, C, dk), F32),
            pltpu.VMEM((H, C, dv), F32),
            pltpu.VMEM((H, C, dv), F32),
        ],
        compiler_params=pltpu.CompilerParams(dimension_semantics=("parallel", "arbitrary")),
        name="gla",
    )(p, p, p, p, p, wup, bias, ng)


def _swa_kernel(sink_ref, q_ref, kp_ref, kc_ref, vp_ref, vc_ref, gate_ref, o_ref):
    W, hd = SWA_WINDOW, SWA_HEAD_DIM
    n = pl.program_id(1)
    kk = jnp.concatenate([kp_ref[...], kc_ref[...]], axis=0)
    vv = jnp.concatenate([vp_ref[...], vc_ref[...]], axis=0)
    lane = lax.broadcasted_iota(jnp.int32, (2 * W, LANES), 1)
    r = lax.broadcasted_iota(jnp.int32, (W, 2 * W), 0)
    c = lax.broadcasted_iota(jnp.int32, (W, 2 * W), 1)
    allowed = (c > r) & (c <= r + W) & ((c >= W) | (n > 0))
    swap = lambda t: jnp.concatenate([t[:, hd:], t[:, :hd]], axis=1)
    group = SWA_Q_HEADS // SWA_KV_HEADS
    for hk in range(SWA_KV_HEADS):
        own = (lane >= hk * hd) & (lane < (hk + 1) * hd)
        k_own = jnp.where(own, kk, jnp.zeros_like(kk))
        v_own = jnp.where(own, vv, jnp.zeros_like(vv))
        k_half = (k_own, swap(k_own)) if hk == 0 else (swap(k_own), k_own)
        v_half = (v_own, swap(v_own)) if hk == 0 else (swap(v_own), v_own)
        for pair in range(group // 2):
            head0 = hk * group + 2 * pair
            col0 = head0 * hd
            qp = q_ref[:, col0:col0 + LANES]
            o_pair = jnp.zeros((W, LANES), F32)
            for half in range(2):
                sink = sink_ref[head0 + half]
                s = _dot_nt(qp, k_half[half]) * (hd ** -0.5)
                s = jnp.where(allowed, s, -1e30)
                m = jnp.maximum(jnp.max(s, axis=-1, keepdims=True), sink)
                p = jnp.exp(s - m)
                den = jnp.sum(p, axis=-1, keepdims=True) + jnp.exp(sink - m)
                p = p * (1.0 / den)
                o_pair = o_pair + jnp.dot(p.astype(BF16), v_half[half], preferred_element_type=F32)
            gate = gate_ref[:, col0:col0 + LANES].astype(F32)
            o_ref[:, col0:col0 + LANES] = (o_pair * _silu(gate)).astype(o_ref.dtype)


def _swa(p, sinks, batch, seq):
    W = SWA_WINDOW
    nb = seq // W
    width = SWA_Q_HEADS * SWA_HEAD_DIM
    q0, g0 = PROJ_OFF[SEG_BQ] // width, PROJ_OFF[SEG_BGATE] // width
    k0, v0 = PROJ_OFF[SEG_BK] // LANES, PROJ_OFF[SEG_BV] // LANES
    cur = lambda b, n: b * nb + n
    prev = lambda b, n: b * nb + jnp.maximum(n - 1, 0)
    return pl.pallas_call(
        _swa_kernel,
        grid=(batch, nb),
        in_specs=[
            pl.BlockSpec(memory_space=pltpu.SMEM),
            pl.BlockSpec((W, width), lambda b, n: (cur(b, n), q0)),
            pl.BlockSpec((W, LANES), lambda b, n: (prev(b, n), k0)),
            pl.BlockSpec((W, LANES), lambda b, n: (cur(b, n), k0)),
            pl.BlockSpec((W, LANES), lambda b, n: (prev(b, n), v0)),
            pl.BlockSpec((W, LANES), lambda b, n: (cur(b, n), v0)),
            pl.BlockSpec((W, width), lambda b, n: (cur(b, n), g0)),
        ],
        out_specs=pl.BlockSpec((W, width), lambda b, n: (cur(b, n), 0)),
        out_shape=jax.ShapeDtypeStruct((batch * seq, width), BF16),
        compiler_params=pltpu.CompilerParams(dimension_semantics=("parallel", "parallel")),
        name="swa",
    )(sinks, p, p, p, p, p, p)


MOBA_PENALTY = 1e30
LOG2_E = 1.4426950408889634


MOBA_HEAD_GROUP = 4


MOBA_KEY_CHUNK = 128


def _moba_kernel(q_ref, k_ref, v_ref, gate_ref, o_ref, kmean_ref, vt_ref, pen_ref, acc_ref, s_ref, p_ref):
    BLK, hd, HG, RC = MOBA_BLOCK, MOBA_HEAD_DIM, MOBA_HEAD_GROUP, MOBA_KEY_CHUNK
    PAIR = 2 * BLK
    n_chunks = PAIR // RC
    i = pl.program_id(2)
    nblk = k_ref.shape[0] // BLK
    heads = [slice(h * hd, (h + 1) * hd) for h in range(HG)]
    c = (hd ** -0.5) * LOG2_E

    @pl.when(i == 0)
    def _():
        for h, cols in enumerate(heads):
            kf = k_ref[:, cols].astype(F32).reshape(nblk, BLK, hd)
            kmean_ref[h] = jnp.sum(kf, axis=1) * (1.0 / BLK)
            for n in range(nblk):
                vt = jnp.transpose(v_ref[n * BLK:(n + 1) * BLK, cols].astype(F32)).astype(BF16)
                vt_ref[h, n // 2, :, (n % 2) * BLK:(n % 2 + 1) * BLK] = vt

    def score_chunk(h, pair, r):
        start = pl.multiple_of(pair * PAIR, PAIR)
        sc = _dot_nt(k_ref[pl.ds(start + r * RC, RC), heads[h]], q_ref[:, heads[h]])
        s_ref[h, r * RC:(r + 1) * RC, :] = sc
        return jnp.max(sc, axis=0, keepdims=True)

    def block_maxes(maxes):
        half = n_chunks // 2
        return functools.reduce(jnp.maximum, maxes[:half]), functools.reduce(jnp.maximum, maxes[half:])

    def softmax_step(h, pair, m_prev, l_prev, m_a, m_b, next_pair=None):
        pen_a = pen_ref[h, pl.ds(2 * pair, 1), :]
        pen_b = pen_ref[h, pl.ds(2 * pair + 1, 1), :]
        m_new = jnp.maximum(m_prev, jnp.maximum(m_a - pen_a, m_b - pen_b))
        alpha = jnp.exp2((m_prev - m_new) * c)
        l_new = alpha * l_prev
        maxes = []
        for r in range(n_chunks):
            off = m_new + (pen_a if r < n_chunks // 2 else pen_b)
            p = jnp.exp2((s_ref[h, r * RC:(r + 1) * RC, :] - off) * c)
            l_new = l_new + jnp.sum(p, axis=0, keepdims=True)
            p_ref[h, r * RC:(r + 1) * RC, :] = p.astype(BF16)
            if next_pair is not None:
                maxes.append(score_chunk(h, next_pair, r))
        acc_ref[h] = alpha * acc_ref[h] + jnp.dot(vt_ref[h, pair], p_ref[h], preferred_element_type=F32)
        if next_pair is None:
            return m_new, l_new
        return (m_new, l_new) + block_maxes(maxes)

    own = pl.multiple_of(i * BLK, BLK)
    key = lax.broadcasted_iota(jnp.int32, (BLK, BLK), 0)
    qry = lax.broadcasted_iota(jnp.int32, (BLK, BLK), 1)
    blk = lax.broadcasted_iota(jnp.int32, (nblk, BLK), 0)
    neg_inf = jnp.float32(-jnp.inf)

    carry = []
    for h, cols in enumerate(heads):
        q = q_ref[:, cols]
        gate = lax.dot_general(kmean_ref[h], q.astype(F32), (((1,), (1,)), ((), ())),
                               preferred_element_type=F32, precision=lax.Precision.HIGHEST)
        g = jnp.where(blk < i, gate, neg_inf)
        sel = jnp.zeros(gate.shape, jnp.bool_)
        for _ in range(MOBA_TOPK):
            m = jnp.max(g, axis=0, keepdims=True)
            idx = jnp.min(jnp.where(g == m, blk, nblk), axis=0, keepdims=True)
            pick = (blk == idx) & (m > neg_inf)
            sel = sel | pick
            g = jnp.where(pick, neg_inf, g)
        pen_ref[h] = jnp.where(sel, 0.0, MOBA_PENALTY)

        s = _dot_nt(k_ref[pl.ds(own, BLK), cols], q)
        s = jnp.where(key <= qry, s, -MOBA_PENALTY)
        m0 = jnp.max(s, axis=0, keepdims=True)
        p = jnp.exp2((s - m0) * c)
        l0 = jnp.sum(p, axis=0, keepdims=True)
        own_vt = vt_ref[h, i // 2, :, pl.ds(pl.multiple_of((i % 2) * BLK, BLK), BLK)]
        acc_ref[h] = jnp.dot(own_vt, p.astype(BF16), preferred_element_type=F32)
        carry.append((m0, l0) + block_maxes([score_chunk(h, 0, r) for r in range(n_chunks)]))

    n_pairs = (i + 1) // 2

    def pipelined(j, carry):
        return tuple(softmax_step(h, j, *carry[h], next_pair=j + 1) for h in range(HG))

    carry = lax.fori_loop(0, jnp.maximum(n_pairs - 1, 0), pipelined, tuple(carry))
    last = jnp.maximum(n_pairs - 1, 0)
    for h, cols in enumerate(heads):
        _, l_fin = softmax_step(h, last, *carry[h])
        o = jnp.transpose(acc_ref[h] * (1.0 / l_fin))
        o_ref[:, cols] = (o * _silu(gate_ref[:, cols].astype(F32))).astype(o_ref.dtype)


def _moba(p, batch, seq):
    BLK, hd, HG = MOBA_BLOCK, MOBA_HEAD_DIM, MOBA_HEAD_GROUP
    nblk = seq // BLK
    assert seq % (2 * BLK) == 0 and MOBA_HEADS % HG == 0
    gw = HG * hd
    q0, k0 = PROJ_OFF[SEG_CQ] // gw, PROJ_OFF[SEG_CK] // gw
    v0, g0 = PROJ_OFF[SEG_CV] // gw, PROJ_OFF[SEG_CGATE] // gw
    scratch_bytes = HG * (seq * hd * 2 + hd * BLK * 4 + 2 * BLK * BLK * 4 + 2 * BLK * BLK * 2)
    vmem = 2 * (2 * seq * gw * 2 + 3 * BLK * gw * 2) + scratch_bytes + (12 << 20)
    return pl.pallas_call(
        _moba_kernel,
        grid=(batch, MOBA_HEADS // HG, nblk),
        in_specs=[
            pl.BlockSpec((BLK, gw), lambda b, h, i: (b * nblk + i, q0 + h)),
            pl.BlockSpec((seq, gw), lambda b, h, i: (b, k0 + h)),
            pl.BlockSpec((seq, gw), lambda b, h, i: (b, v0 + h)),
            pl.BlockSpec((BLK, gw), lambda b, h, i: (b * nblk + i, g0 + h)),
        ],
        out_specs=pl.BlockSpec((BLK, gw), lambda b, h, i: (b * nblk + i, h)),
        out_shape=jax.ShapeDtypeStruct((batch * seq, MOBA_HEADS * hd), BF16),
        scratch_shapes=[
            pltpu.VMEM((HG, nblk, hd), F32),
            pltpu.VMEM((HG, nblk // 2, hd, 2 * BLK), BF16),
            pltpu.VMEM((HG, nblk, BLK), F32),
            pltpu.VMEM((HG, hd, BLK), F32),
            pltpu.VMEM((HG, 2 * BLK, BLK), F32),
            pltpu.VMEM((HG, 2 * BLK, BLK), BF16),
        ],
        compiler_params=pltpu.CompilerParams(
            dimension_semantics=("parallel", "parallel", "arbitrary"), vmem_limit_bytes=vmem),
        name="moba",
    )(p, p, p, p)


MERGE_TILE_M = 256


def _merge_kernel(alpha, ya_ref, yb_ref, yc_ref, mg_ref, x_ref, wb_ref, bm_ref, wo_ref, lg_ref, lb_ref,
                  y_ref, yb16_ref):
    D = D_MODEL
    merged = None
    for n, y_n in enumerate((ya_ref, yb_ref, yc_ref)):
        up = jnp.dot(y_n[...], wb_ref[n], preferred_element_type=F32)
        gate = _sigmoid(mg_ref[:, n * D:(n + 1) * D].astype(F32) + bm_ref[n:n + 1, :])
        merged = gate * up if merged is None else merged + gate * up
    out = jnp.dot(merged.astype(BF16), wo_ref[...], preferred_element_type=F32)
    h = alpha * x_ref[...] + out
    mu = jnp.mean(h, axis=-1, keepdims=True)
    hc = h - mu
    var = jnp.mean(hc * hc, axis=-1, keepdims=True)
    y = hc * lax.rsqrt(var + LN_EPS) * lg_ref[...] + lb_ref[...]
    y_ref[...] = y
    yb16_ref[...] = y.astype(BF16)


def _merge(ya, yb, yc, p, x, wb, bm, wo, lg, lb, alpha):
    m = x.shape[0]
    D, Wd = D_MODEL, BRANCH_WIDTH
    tm = min(MERGE_TILE_M, m)
    once = pl.Buffered(1)
    resident = (N_BRANCH * Wd * D + D * D) * 2
    streamed = 2 * (3 * tm * Wd * 2 + tm * 3 * D * 2 + tm * D * 4 + tm * D * 4 + tm * D * 2)
    vmem = resident + streamed + (12 << 20)
    row = lambda i: (i, 0)
    fixed2 = lambda i: (0, 0)
    return pl.pallas_call(
        functools.partial(_merge_kernel, alpha),
        grid=(m // tm,),
        in_specs=[
            pl.BlockSpec((tm, Wd), row), pl.BlockSpec((tm, Wd), row), pl.BlockSpec((tm, Wd), row),
            pl.BlockSpec((tm, N_BRANCH * D), lambda i: (i, PROJ_OFF[SEG_MGATE] // (N_BRANCH * D))),
            pl.BlockSpec((tm, D), row),
            pl.BlockSpec((N_BRANCH, Wd, D), lambda i: (0, 0, 0), pipeline_mode=once),
            pl.BlockSpec((N_BRANCH, D), fixed2),
            pl.BlockSpec((D, D), fixed2, pipeline_mode=once),
            pl.BlockSpec((1, D), fixed2), pl.BlockSpec((1, D), fixed2),
        ],
        out_specs=[pl.BlockSpec((tm, D), row), pl.BlockSpec((tm, D), row)],
        out_shape=[jax.ShapeDtypeStruct((m, D), F32), jax.ShapeDtypeStruct((m, D), BF16)],
        compiler_params=pltpu.CompilerParams(
            dimension_semantics=("parallel",), vmem_limit_bytes=min(vmem, V7X_VMEM_BYTES - (4 << 20))),
        name="merge_out_ln",
    )(ya, yb, yc, p, x, wb, bm, wo, lg, lb)


def kernel(x, w_in, gla_w_up, gla_b, gla_norm_g, swa_sinks, b_merge, w_branch, w_o, ln_g, ln_b):
    batch, seq, d = x.shape
    depth = w_in.shape[0]
    alpha = (2 * depth) ** 0.25
    assert d == D_MODEL and seq % (2 * MOBA_BLOCK) == 0 and seq % GLA_CHUNK_ROWS == 0

    w_in_p = _permute_w_in(w_in)
    wup = jnp.pad(gla_w_up, ((0, 0), (0, LANES - GLA_RANK), (0, 0))).astype(BF16)
    wb = w_branch.astype(BF16)
    wo = w_o.astype(BF16)

    xf = x.reshape(batch * seq, d)
    xb = xf.astype(BF16)
    for l in range(depth):
        p = _proj(xb, w_in_p[l])
        ya = _gla(p, wup[l], gla_b[l][None, :], gla_norm_g[l][None, :], batch, seq)
        yb = _swa(p, swa_sinks[l], batch, seq)
        yc = _moba(p, batch, seq)
        xf, xb = _merge(ya, yb, yc, p, xf, wb[l], b_merge[l], wo[l], ln_g[l][None, :], ln_b[l][None, :],
                        alpha)
    return xf.reshape(batch, seq, d)
```

```python
import functools

import jax
import jax.numpy as jnp
import numpy as np
from jax import lax
from jax.experimental import pallas as pl
from jax.experimental.pallas import tpu as pltpu

F32 = jnp.float32
BF16 = jnp.bfloat16

D_MODEL = 2048
BRANCH_WIDTH = 1024
N_BRANCH = 3
GLA_HEADS, GLA_DK, GLA_DV, GLA_RANK, GLA_TAU = 4, 128, 256, 16, 16.0
SWA_Q_HEADS, SWA_KV_HEADS, SWA_HEAD_DIM, SWA_WINDOW = 16, 2, 64, 128
MOBA_HEADS, MOBA_HEAD_DIM, MOBA_BLOCK, MOBA_TOPK = 8, 128, 256, 3
LN_EPS = 1e-5
RMS_EPS = 1e-6

IN_SPLITS = (
    GLA_HEADS * GLA_DK, GLA_HEADS * GLA_DK, GLA_HEADS * GLA_DV, GLA_RANK, BRANCH_WIDTH,
    SWA_Q_HEADS * SWA_HEAD_DIM, SWA_KV_HEADS * SWA_HEAD_DIM, SWA_KV_HEADS * SWA_HEAD_DIM, BRANCH_WIDTH,
    MOBA_HEADS * MOBA_HEAD_DIM, MOBA_HEADS * MOBA_HEAD_DIM, MOBA_HEADS * MOBA_HEAD_DIM, BRANCH_WIDTH,
    N_BRANCH * D_MODEL,
)
(SEG_AQ, SEG_AK, SEG_AV, SEG_ALR, SEG_AGATE, SEG_BQ, SEG_BK, SEG_BV, SEG_BGATE,
 SEG_CQ, SEG_CK, SEG_CV, SEG_CGATE, SEG_MGATE) = range(14)

LANES = 128
V7X_VMEM_BYTES = 64 * 1024 * 1024

PROJ_ORDER = (SEG_MGATE, SEG_AV, SEG_AGATE, SEG_BQ, SEG_BGATE, SEG_CQ, SEG_CK, SEG_CV, SEG_CGATE,
              SEG_AQ, SEG_AK, SEG_BK, SEG_BV, SEG_ALR)
PROJ_TILE_N = 512


def _proj_layout():
    src = np.concatenate([[0], np.cumsum(IN_SPLITS)])
    offs, pieces, cur = {}, [], 0
    for seg in PROJ_ORDER:
        width = IN_SPLITS[seg]
        padded = -(-width // LANES) * LANES
        offs[seg] = cur
        pieces.append((int(src[seg]), width, padded - width))
        cur += padded
    total = -(-cur // PROJ_TILE_N) * PROJ_TILE_N
    return offs, pieces, cur, total


PROJ_OFF, PROJ_PIECES, PROJ_USED, PROJ_COLS = _proj_layout()


PERMUTE_TILE_LANES = 128
PERMUTE_CHUNK_ROWS = 512


def _permute_kernel(wt_ref, o_ref):
    dst = 0
    for src, width, pad in PROJ_PIECES:
        for c0 in range(0, width, PERMUTE_CHUNK_ROWS):
            w = min(PERMUTE_CHUNK_ROWS, width - c0)
            o_ref[dst + c0:dst + c0 + w, :] = wt_ref[src + c0:src + c0 + w, :].astype(o_ref.dtype)
        if pad:
            o_ref[dst + width:dst + width + pad, :] = jnp.zeros((pad, o_ref.shape[1]), o_ref.dtype)
        dst += width + pad
    if PROJ_COLS > dst:
        o_ref[dst:, :] = jnp.zeros((PROJ_COLS - dst, o_ref.shape[1]), o_ref.dtype)


def _permute_w_in(w_in):
    depth, d, d_in = w_in.shape
    tl = PERMUTE_TILE_LANES
    vmem = 2 * (d_in * tl * 4 + PROJ_COLS * tl * 2) + (8 << 20)
    return pl.pallas_call(
        _permute_kernel,
        grid=(depth, d // tl),
        in_specs=[pl.BlockSpec((None, d_in, tl), lambda l, i: (l, 0, i))],
        out_specs=pl.BlockSpec((None, PROJ_COLS, tl), lambda l, i: (l, 0, i)),
        out_shape=jax.ShapeDtypeStruct((depth, PROJ_COLS, d), BF16),
        compiler_params=pltpu.CompilerParams(
            dimension_semantics=("parallel", "parallel"), vmem_limit_bytes=vmem),
        name="permute_w_in",
    )(jnp.swapaxes(w_in, 1, 2))


def _silu(x):
    return x * (1.0 / (1.0 + jnp.exp(-x)))


def _sigmoid(x):
    return 1.0 / (1.0 + jnp.exp(-x))


def _dot_nt(a, b):
    return lax.dot_general(a, b, (((1,), (1,)), ((), ())), preferred_element_type=F32)


PROJ_TILE_M = 2048


def _proj_kernel(x_ref, wt_ref, o_ref):
    o_ref[...] = _dot_nt(x_ref[...], wt_ref[...]).astype(o_ref.dtype)


def _proj(xb, wt, layer):
    m, d = xb.shape
    n = wt.shape[1]
    tm, tn = min(PROJ_TILE_M, m), PROJ_TILE_N
    vmem = 2 * (tm * d * 2 + d * tn * 2 + tm * tn * 2) + (8 << 20)
    return pl.pallas_call(
        _proj_kernel,
        grid=(m // tm, n // tn),
        in_specs=[pl.BlockSpec((tm, d), lambda i, j: (i, 0)),
                  pl.BlockSpec((None, tn, d), lambda i, j: (layer, j, 0))],
        out_specs=pl.BlockSpec((tm, tn), lambda i, j: (i, j)),
        out_shape=jax.ShapeDtypeStruct((m, n), BF16),
        compiler_params=pltpu.CompilerParams(
            dimension_semantics=("parallel", "arbitrary"), vmem_limit_bytes=vmem),
        name="in_proj",
    )(xb, wt)


GLA_CHUNK_ROWS = 128
GLA_SUB = 16
GLA_FAST_MAX_DECAY = 30.0
GLA_FAST_MIN_Q = 1e-20
GLA_FAST_MAX_QK = 1e20


def _gla_kernel(q_ref, k_ref, v_ref, gate_ref, alr_ref, wup_ref, bias_ref, ng_ref, o_ref,
                s_ref, b_ref, qs_ref, qx_ref, kf_ref, vf_ref, acc_ref):
    C, H, dk, dv = GLA_CHUNK_ROWS, GLA_HEADS, GLA_DK, GLA_DV

    @pl.when(pl.program_id(1) == 0)
    def _():
        s_ref[...] = jnp.zeros_like(s_ref)

    z = jnp.dot(alr_ref[...], wup_ref[...], preferred_element_type=F32) + bias_ref[...]
    g = -(jnp.maximum(-z, 0.0) + jnp.log1p(jnp.exp(-jnp.abs(z)))) * (1.0 / GLA_TAU)

    row = lax.broadcasted_iota(jnp.int32, (C, C), 0)
    col = lax.broadcasted_iota(jnp.int32, (C, C), 1)
    tril = jnp.where(col <= row, 1.0, 0.0).astype(BF16)
    g1 = g.astype(BF16)
    r1 = g - g1.astype(F32)
    g2 = r1.astype(BF16)
    g3 = (r1 - g2.astype(F32)).astype(BF16)
    b_all = (jnp.dot(tril, g1, preferred_element_type=F32)
             + jnp.dot(tril, g2, preferred_element_type=F32)
             + jnp.dot(tril, g3, preferred_element_type=F32))

    q_max = jnp.float32(0.0)
    k_max = jnp.float32(0.0)
    for h in range(H):
        b = b_all[:, h * dk:(h + 1) * dk]
        qs = q_ref[:, h * dk:(h + 1) * dk].astype(F32) * (dk ** -0.5)
        kf = k_ref[:, h * dk:(h + 1) * dk].astype(F32)
        qx = (qs * jnp.exp(b)).astype(BF16)
        b_ref[h] = b
        qs_ref[h] = qs
        kf_ref[h] = kf
        qx_ref[h] = qx
        q_max = jnp.maximum(q_max, jnp.max(jnp.abs(qs)))
        k_max = jnp.maximum(k_max, jnp.max(jnp.abs(kf)))
        acc_ref[h] = jnp.dot(qx, s_ref[h].astype(BF16), preferred_element_type=F32)

    total_decay = jnp.max(-b_all[C - 1:C, :])
    in_range = ((total_decay <= GLA_FAST_MAX_DECAY) & (q_max >= GLA_FAST_MIN_Q)
                & (q_max <= GLA_FAST_MAX_QK) & (k_max <= GLA_FAST_MAX_QK))

    @pl.when(in_range)
    def _():
        for h in range(H):
            k_inv = (kf_ref[h] * jnp.exp(-b_ref[h])).astype(BF16)
            a = jnp.where(col <= row, _dot_nt(qx_ref[h], k_inv), 0.0)
            acc_ref[h] += jnp.dot(a.astype(BF16), v_ref[:, h * dv:(h + 1) * dv],
                                  preferred_element_type=F32)

    @pl.when(jnp.logical_not(in_range))
    def _():
        _gla_sub_blocks(v_ref, b_ref, qs_ref, kf_ref, vf_ref, acc_ref)

    for h in range(H):
        b = b_ref[h]
        b_end = b[C - 1:C, :]
        k_end = kf_ref[h] * jnp.exp(b_end - b)
        decay_col = jnp.transpose(jnp.broadcast_to(jnp.exp(b_end), (dk, dk)))
        decay = jnp.concatenate([decay_col] * (dv // dk), axis=1)
        s_ref[h] = s_ref[h] * decay + jnp.dot(jnp.transpose(k_end).astype(BF16),
                                              v_ref[:, h * dv:(h + 1) * dv],
                                              preferred_element_type=F32)
        o = acc_ref[h]
        o = o * lax.rsqrt(jnp.mean(o * o, axis=-1, keepdims=True) + RMS_EPS) * ng_ref[...]
        gate = gate_ref[:, h * dv:(h + 1) * dv].astype(F32)
        o_ref[:, h * dv:(h + 1) * dv] = (o * _silu(gate)).astype(o_ref.dtype)


def _gla_sub_blocks(v_ref, b_ref, qs_ref, kf_ref, vf_ref, acc_ref):
    C, H, dk, dv = GLA_CHUNK_ROWS, GLA_HEADS, GLA_DK, GLA_DV
    for h in range(H):
        vf_ref[h] = v_ref[:, h * dv:(h + 1) * dv].astype(F32)
    key_idx = lax.broadcasted_iota(jnp.int32, (GLA_SUB, C), 1)
    sub_row = lax.broadcasted_iota(jnp.int32, (GLA_SUB, dk), 0)

    def sub_block(i, carry):
        base = pl.multiple_of(i * GLA_SUB, GLA_SUB)
        prev = jnp.maximum(base - 1, 0)
        for h in range(H):
            b_start = b_ref[h, pl.ds(prev, 1), :]
            b_i = b_ref[h, pl.ds(base, GLA_SUB), :]
            qs_i = qs_ref[h, pl.ds(base, GLA_SUB), :]
            kx = kf_ref[h] * jnp.exp(jnp.minimum(b_start - b_ref[h], 0.0))
            qx = qs_i * jnp.exp(jnp.minimum(b_i - b_start, 0.0))
            s = _dot_nt(qx.astype(BF16), kx.astype(BF16))
            s = jnp.where(key_idx < base, s, 0.0)
            o_past = jnp.dot(s.astype(BF16), v_ref[:, h * dv:(h + 1) * dv], preferred_element_type=F32)
            terms = []
            for j in range(GLA_SUB):
                b_j = b_ref[h, pl.ds(base + j, 1), :]
                k_j = kf_ref[h, pl.ds(base + j, 1), :]
                v_j = vf_ref[h, pl.ds(base + j, 1), :]
                t = jnp.exp(jnp.minimum(b_i - b_j, 0.0)) * (qs_i * k_j)
                t = jnp.where(sub_row >= j, t, 0.0)
                terms.append(jnp.sum(t, axis=-1, keepdims=True) * v_j)
            while len(terms) > 1:
                terms = [a + b for a, b in zip(terms[0::2], terms[1::2])]
            acc_ref[h, pl.ds(base, GLA_SUB), :] += terms[0] + o_past
        return carry

    lax.fori_loop(0, C // GLA_SUB, sub_block, 0)


def _gla(p, wup, bias, ng, batch, seq, layer):
    C, H, dk, dv = GLA_CHUNK_ROWS, GLA_HEADS, GLA_DK, GLA_DV
    nc = seq // C
    rows = lambda b, c: b * nc + c
    q0, k0 = PROJ_OFF[SEG_AQ] // (H * dk), PROJ_OFF[SEG_AK] // (H * dk)
    v0, g0 = PROJ_OFF[SEG_AV] // (H * dv), PROJ_OFF[SEG_AGATE] // (H * dv)
    lr0 = PROJ_OFF[SEG_ALR] // LANES
    return pl.pallas_call(
        _gla_kernel,
        grid=(batch, nc),
        in_specs=[
            pl.BlockSpec((C, H * dk), lambda b, c: (rows(b, c), q0)),
            pl.BlockSpec((C, H * dk), lambda b, c: (rows(b, c), k0)),
            pl.BlockSpec((C, H * dv), lambda b, c: (rows(b, c), v0)),
            pl.BlockSpec((C, H * dv), lambda b, c: (rows(b, c), g0)),
            pl.BlockSpec((C, LANES), lambda b, c: (rows(b, c), lr0)),
            pl.BlockSpec((None, LANES, H * dk), lambda b, c: (layer, 0, 0)),
            pl.BlockSpec((None, 1, H * dk), lambda b, c: (layer, 0, 0)),
            pl.BlockSpec((None, 1, dv), lambda b, c: (layer, 0, 0)),
        ],
        out_specs=pl.BlockSpec((C, H * dv), lambda b, c: (rows(b, c), 0)),
        out_shape=jax.ShapeDtypeStruct((batch * seq, H * dv), BF16),
        scratch_shapes=[
            pltpu.VMEM((H, dk, dv), F32),
            pltpu.VMEM((H, C, dk), F32),
            pltpu.VMEM((H, C, dk), F32),
            pltpu.VMEM((H, C, dk), BF16),
            pltpu.VMEM((H, C, dk), F32),
            pltpu.VMEM((H, C, dv), F32),
            pltpu.VMEM((H, C, dv), F32),
        ],
        compiler_params=pltpu.CompilerParams(dimension_semantics=("parallel", "arbitrary")),
        name="gla",
    )(p, p, p, p, p, wup, bias, ng)


def _swa_kernel(layer, sink_ref, q_ref, kp_ref, kc_ref, vp_ref, vc_ref, gate_ref, o_ref, s_ref, p_ref):
    W, hd = SWA_WINDOW, SWA_HEAD_DIM
    n = pl.program_id(1)
    kk = jnp.concatenate([kp_ref[...], kc_ref[...]], axis=0)
    vv = jnp.concatenate([vp_ref[...], vc_ref[...]], axis=0)
    vt_all = jnp.transpose(vv.astype(F32))
    key_lane = lax.broadcasted_iota(jnp.int32, (2 * W, LANES), 1)
    q_lane = lax.broadcasted_iota(jnp.int32, (W, LANES), 1)
    key = lax.broadcasted_iota(jnp.int32, (2 * W, 2 * W), 0)
    col = lax.broadcasted_iota(jnp.int32, (2 * W, 2 * W), 1)
    qry = jnp.where(col >= W, col - W, col)
    allowed = (key > qry) & (key <= qry + W) & ((key >= W) | (n > 0))
    second = lax.broadcasted_iota(jnp.int32, (1, 2 * W), 1) >= W
    swap = lambda t: jnp.concatenate([t[:, hd:], t[:, :hd]], axis=1)
    low_half = lambda t, lane, first: (jnp.where(lane < hd, t, jnp.zeros_like(t)) if first
                                       else swap(jnp.where(lane >= hd, t, jnp.zeros_like(t))))
    c = (hd ** -0.5) * LOG2_E
    inv_scale = float(hd) ** 0.5
    group = SWA_Q_HEADS // SWA_KV_HEADS
    for hk in range(SWA_KV_HEADS):
        k_lo = low_half(kk, key_lane, hk == 0)
        vt = vt_all[hk * hd:(hk + 1) * hd, :].astype(BF16)
        for pair in range(group // 2):
            head0 = hk * group + 2 * pair
            col0 = head0 * hd
            qp = q_ref[:, col0:col0 + LANES]
            q2 = jnp.concatenate([low_half(qp, q_lane, True), low_half(qp, q_lane, False)], axis=0)
            slot = hk * (group // 2) + pair
            sink = jnp.where(second, sink_ref[layer, head0 + 1], sink_ref[layer, head0]) * inv_scale
            m = sink
            for r in range(2):
                rows = slice(r * W, (r + 1) * W)
                sc = jnp.where(allowed[rows], _dot_nt(k_lo[rows], q2), -1e30)
                s_ref[slot, rows, :] = sc
                m = jnp.maximum(m, jnp.max(sc, axis=0, keepdims=True))
            den = jnp.exp2((sink - m) * c)
            for r in range(2):
                rows = slice(r * W, (r + 1) * W)
                p = jnp.exp2((s_ref[slot, rows, :] - m) * c)
                den = den + jnp.sum(p, axis=0, keepdims=True)
                p_ref[slot, rows, :] = p.astype(BF16)
            o_t = jnp.dot(vt, p_ref[slot], preferred_element_type=F32) * (1.0 / den)
            o = jnp.transpose(jnp.concatenate([o_t[:, :W], o_t[:, W:]], axis=0))
            gate = gate_ref[:, col0:col0 + LANES].astype(F32)
            o_ref[:, col0:col0 + LANES] = (o * _silu(gate)).astype(o_ref.dtype)


def _swa(p, sinks, batch, seq, layer):
    W = SWA_WINDOW
    nb = seq // W
    width = SWA_Q_HEADS * SWA_HEAD_DIM
    q0, g0 = PROJ_OFF[SEG_BQ] // width, PROJ_OFF[SEG_BGATE] // width
    k0, v0 = PROJ_OFF[SEG_BK] // LANES, PROJ_OFF[SEG_BV] // LANES
    cur = lambda b, n: b * nb + n
    prev = lambda b, n: b * nb + jnp.maximum(n - 1, 0)
    return pl.pallas_call(
        functools.partial(_swa_kernel, layer),
        grid=(batch, nb),
        in_specs=[
            pl.BlockSpec(memory_space=pltpu.SMEM),
            pl.BlockSpec((W, width), lambda b, n: (cur(b, n), q0)),
            pl.BlockSpec((W, LANES), lambda b, n: (prev(b, n), k0)),
            pl.BlockSpec((W, LANES), lambda b, n: (cur(b, n), k0)),
            pl.BlockSpec((W, LANES), lambda b, n: (prev(b, n), v0)),
            pl.BlockSpec((W, LANES), lambda b, n: (cur(b, n), v0)),
            pl.BlockSpec((W, width), lambda b, n: (cur(b, n), g0)),
        ],
        out_specs=pl.BlockSpec((W, width), lambda b, n: (cur(b, n), 0)),
        out_shape=jax.ShapeDtypeStruct((batch * seq, width), BF16),
        scratch_shapes=[
            pltpu.VMEM((SWA_Q_HEADS // 2, 2 * W, 2 * W), F32),
            pltpu.VMEM((SWA_Q_HEADS // 2, 2 * W, 2 * W), BF16),
        ],
        compiler_params=pltpu.CompilerParams(dimension_semantics=("parallel", "parallel")),
        name="swa",
    )(sinks, p, p, p, p, p, p)


MOBA_PENALTY = 1e30
LOG2_E = 1.4426950408889634


MOBA_HEAD_GROUP = 4


MOBA_KEY_CHUNK = 128


def _moba_kernel(q_ref, k_ref, v_ref, gate_ref, o_ref, kmean_ref, vt_ref, pen_ref, acc_ref, s_ref, p_ref):
    BLK, hd, HG, RC = MOBA_BLOCK, MOBA_HEAD_DIM, MOBA_HEAD_GROUP, MOBA_KEY_CHUNK
    PAIR = 2 * BLK
    n_chunks = PAIR // RC
    i = pl.program_id(2)
    nblk = k_ref.shape[0] // BLK
    heads = [slice(h * hd, (h + 1) * hd) for h in range(HG)]
    c = (hd ** -0.5) * LOG2_E

    @pl.when(i == 0)
    def _():
        for h, cols in enumerate(heads):
            kf = k_ref[:, cols].astype(F32).reshape(nblk, BLK, hd)
            kmean_ref[h] = jnp.sum(kf, axis=1) * (1.0 / BLK)
            for n in range(nblk):
                vt = jnp.transpose(v_ref[n * BLK:(n + 1) * BLK, cols].astype(F32)).astype(BF16)
                vt_ref[h, n // 2, :, (n % 2) * BLK:(n % 2 + 1) * BLK] = vt

    def score_chunk(h, pair, r):
        start = pl.multiple_of(pair * PAIR, PAIR)
        sc = _dot_nt(k_ref[pl.ds(start + r * RC, RC), heads[h]], q_ref[:, heads[h]])
        s_ref[h, r * RC:(r + 1) * RC, :] = sc
        return jnp.max(sc, axis=0, keepdims=True)

    def block_maxes(maxes):
        half = n_chunks // 2
        return functools.reduce(jnp.maximum, maxes[:half]), functools.reduce(jnp.maximum, maxes[half:])

    def softmax_step(h, pair, m_prev, l_prev, m_a, m_b, next_pair=None):
        pen_a = pen_ref[h, pl.ds(2 * pair, 1), :]
        pen_b = pen_ref[h, pl.ds(2 * pair + 1, 1), :]
        m_new = jnp.maximum(m_prev, jnp.maximum(m_a - pen_a, m_b - pen_b))
        alpha = jnp.exp2((m_prev - m_new) * c)
        l_new = alpha * l_prev
        maxes = []
        for r in range(n_chunks):
            off = m_new + (pen_a if r < n_chunks // 2 else pen_b)
            p = jnp.exp2((s_ref[h, r * RC:(r + 1) * RC, :] - off) * c)
            l_new = l_new + jnp.sum(p, axis=0, keepdims=True)
            p_ref[h, r * RC:(r + 1) * RC, :] = p.astype(BF16)
            if next_pair is not None:
                maxes.append(score_chunk(h, next_pair, r))
        acc_ref[h] = alpha * acc_ref[h] + jnp.dot(vt_ref[h, pair], p_ref[h], preferred_element_type=F32)
        if next_pair is None:
            return m_new, l_new
        return (m_new, l_new) + block_maxes(maxes)

    own = pl.multiple_of(i * BLK, BLK)
    key = lax.broadcasted_iota(jnp.int32, (BLK, BLK), 0)
    qry = lax.broadcasted_iota(jnp.int32, (BLK, BLK), 1)
    blk = lax.broadcasted_iota(jnp.int32, (nblk, BLK), 0)
    neg_inf = jnp.float32(-jnp.inf)

    carry = []
    for h, cols in enumerate(heads):
        q = q_ref[:, cols]
        gate = lax.dot_general(kmean_ref[h], q.astype(F32), (((1,), (1,)), ((), ())),
                               preferred_element_type=F32, precision=lax.Precision.HIGHEST)
        g = jnp.where(blk < i, gate, neg_inf)
        sel = jnp.zeros(gate.shape, jnp.bool_)
        for _ in range(MOBA_TOPK):
            m = jnp.max(g, axis=0, keepdims=True)
            idx = jnp.min(jnp.where(g == m, blk, nblk), axis=0, keepdims=True)
            pick = (blk == idx) & (m > neg_inf)
            sel = sel | pick
            g = jnp.where(pick, neg_inf, g)
        pen_ref[h] = jnp.where(sel, 0.0, MOBA_PENALTY)

        s = _dot_nt(k_ref[pl.ds(own, BLK), cols], q)
        s = jnp.where(key <= qry, s, -MOBA_PENALTY)
        m0 = jnp.max(s, axis=0, keepdims=True)
        p = jnp.exp2((s - m0) * c)
        l0 = jnp.sum(p, axis=0, keepdims=True)
        own_vt = vt_ref[h, i // 2, :, pl.ds(pl.multiple_of((i % 2) * BLK, BLK), BLK)]
        acc_ref[h] = jnp.dot(own_vt, p.astype(BF16), preferred_element_type=F32)
        carry.append((m0, l0) + block_maxes([score_chunk(h, 0, r) for r in range(n_chunks)]))

    n_pairs = (i + 1) // 2

    def pipelined(j, carry):
        return tuple(softmax_step(h, j, *carry[h], next_pair=j + 1) for h in range(HG))

    carry = lax.fori_loop(0, jnp.maximum(n_pairs - 1, 0), pipelined, tuple(carry))
    last = jnp.maximum(n_pairs - 1, 0)
    for h, cols in enumerate(heads):
        _, l_fin = softmax_step(h, last, *carry[h])
        o = jnp.transpose(acc_ref[h] * (1.0 / l_fin))
        o_ref[:, cols] = (o * _silu(gate_ref[:, cols].astype(F32))).astype(o_ref.dtype)


def _moba(p, batch, seq):
    BLK, hd, HG = MOBA_BLOCK, MOBA_HEAD_DIM, MOBA_HEAD_GROUP
    nblk = seq // BLK
    assert seq % (2 * BLK) == 0 and MOBA_HEADS % HG == 0
    gw = HG * hd
    q0, k0 = PROJ_OFF[SEG_CQ] // gw, PROJ_OFF[SEG_CK] // gw
    v0, g0 = PROJ_OFF[SEG_CV] // gw, PROJ_OFF[SEG_CGATE] // gw
    scratch_bytes = HG * (seq * hd * 2 + hd * BLK * 4 + 2 * BLK * BLK * 4 + 2 * BLK * BLK * 2)
    vmem = 2 * (2 * seq * gw * 2 + 3 * BLK * gw * 2) + scratch_bytes + (12 << 20)
    return pl.pallas_call(
        _moba_kernel,
        grid=(batch, MOBA_HEADS // HG, nblk),
        in_specs=[
            pl.BlockSpec((BLK, gw), lambda b, h, i: (b * nblk + i, q0 + h)),
            pl.BlockSpec((seq, gw), lambda b, h, i: (b, k0 + h)),
            pl.BlockSpec((seq, gw), lambda b, h, i: (b, v0 + h)),
            pl.BlockSpec((BLK, gw), lambda b, h, i: (b * nblk + i, g0 + h)),
        ],
        out_specs=pl.BlockSpec((BLK, gw), lambda b, h, i: (b * nblk + i, h)),
        out_shape=jax.ShapeDtypeStruct((batch * seq, MOBA_HEADS * hd), BF16),
        scratch_shapes=[
            pltpu.VMEM((HG, nblk, hd), F32),
            pltpu.VMEM((HG, nblk // 2, hd, 2 * BLK), BF16),
            pltpu.VMEM((HG, nblk, BLK), F32),
            pltpu.VMEM((HG, hd, BLK), F32),
            pltpu.VMEM((HG, 2 * BLK, BLK), F32),
            pltpu.VMEM((HG, 2 * BLK, BLK), BF16),
        ],
        compiler_params=pltpu.CompilerParams(
            dimension_semantics=("parallel", "parallel", "arbitrary"), vmem_limit_bytes=vmem),
        name="moba",
    )(p, p, p, p)


MERGE_TILE_M = 256


def _merge_kernel(alpha, ya_ref, yb_ref, yc_ref, mg_ref, x_ref, wb_ref, bm_ref, wo_ref, lg_ref, lb_ref,
                  y_ref, yb16_ref):
    D = D_MODEL
    merged = None
    for n, y_n in enumerate((ya_ref, yb_ref, yc_ref)):
        up = jnp.dot(y_n[...], wb_ref[n], preferred_element_type=F32)
        gate = _sigmoid(mg_ref[:, n * D:(n + 1) * D].astype(F32) + bm_ref[n:n + 1, :])
        merged = gate * up if merged is None else merged + gate * up
    out = jnp.dot(merged.astype(BF16), wo_ref[...], preferred_element_type=F32)
    h = alpha * x_ref[...] + out
    mu = jnp.mean(h, axis=-1, keepdims=True)
    hc = h - mu
    var = jnp.mean(hc * hc, axis=-1, keepdims=True)
    y = hc * lax.rsqrt(var + LN_EPS) * lg_ref[...] + lb_ref[...]
    y_ref[...] = y
    yb16_ref[...] = y.astype(BF16)


def _merge(ya, yb, yc, p, x, wb, bm, wo, lg, lb, alpha, layer):
    m = x.shape[0]
    D, Wd = D_MODEL, BRANCH_WIDTH
    tm = min(MERGE_TILE_M, m)
    once = pl.Buffered(1)
    resident = (N_BRANCH * Wd * D + D * D) * 2
    streamed = 2 * (3 * tm * Wd * 2 + tm * 3 * D * 2 + tm * D * 4 + tm * D * 4 + tm * D * 2)
    vmem = resident + streamed + (12 << 20)
    row = lambda i: (i, 0)
    return pl.pallas_call(
        functools.partial(_merge_kernel, alpha),
        grid=(m // tm,),
        in_specs=[
            pl.BlockSpec((tm, Wd), row), pl.BlockSpec((tm, Wd), row), pl.BlockSpec((tm, Wd), row),
            pl.BlockSpec((tm, N_BRANCH * D), lambda i: (i, PROJ_OFF[SEG_MGATE] // (N_BRANCH * D))),
            pl.BlockSpec((tm, D), row),
            pl.BlockSpec((None, N_BRANCH, Wd, D), lambda i: (layer, 0, 0, 0), pipeline_mode=once),
            pl.BlockSpec((None, N_BRANCH, D), lambda i: (layer, 0, 0)),
            pl.BlockSpec((None, D, D), lambda i: (layer, 0, 0), pipeline_mode=once),
            pl.BlockSpec((None, 1, D), lambda i: (layer, 0, 0)),
            pl.BlockSpec((None, 1, D), lambda i: (layer, 0, 0)),
        ],
        out_specs=[pl.BlockSpec((tm, D), row), pl.BlockSpec((tm, D), row)],
        out_shape=[jax.ShapeDtypeStruct((m, D), F32), jax.ShapeDtypeStruct((m, D), BF16)],
        compiler_params=pltpu.CompilerParams(
            dimension_semantics=("parallel",), vmem_limit_bytes=min(vmem, V7X_VMEM_BYTES - (4 << 20))),
        name="merge_out_ln",
    )(ya, yb, yc, p, x, wb, bm, wo, lg, lb)


def kernel(x, w_in, gla_w_up, gla_b, gla_norm_g, swa_sinks, b_merge, w_branch, w_o, ln_g, ln_b):
    batch, seq, d = x.shape
    depth = w_in.shape[0]
    alpha = (2 * depth) ** 0.25
    assert d == D_MODEL and seq % (2 * MOBA_BLOCK) == 0 and seq % GLA_CHUNK_ROWS == 0

    w_in_t = _permute_w_in(w_in)
    wup = jnp.pad(gla_w_up, ((0, 0), (0, LANES - GLA_RANK), (0, 0))).astype(BF16)
    wb = w_branch.astype(BF16)
    wo = w_o.astype(BF16)
    gla_bias, gla_ng = gla_b[:, None, :], gla_norm_g[:, None, :]
    lg, lb = ln_g[:, None, :], ln_b[:, None, :]

    xf = x.reshape(batch * seq, d)
    xb = xf.astype(BF16)
    for l in range(depth):
        p = _proj(xb, w_in_t, l)
        ya = _gla(p, wup, gla_bias, gla_ng, batch, seq, l)
        yb = _swa(p, swa_sinks, batch, seq, l)
        yc = _moba(p, batch, seq)
        xf, xb = _merge(ya, yb, yc, p, xf, wb, b_merge, wo, lg, lb, alpha, l)
    return xf.reshape(batch, seq, d)
```

```python
import functools

import jax
import jax.numpy as jnp
import numpy as np
from jax import lax
from jax.experimental import pallas as pl
from jax.experimental.pallas import tpu as pltpu

F32 = jnp.float32
BF16 = jnp.bfloat16

D_MODEL = 2048
BRANCH_WIDTH = 1024
N_BRANCH = 3
GLA_HEADS, GLA_DK, GLA_DV, GLA_RANK, GLA_TAU = 4, 128, 256, 16, 16.0
SWA_Q_HEADS, SWA_KV_HEADS, SWA_HEAD_DIM, SWA_WINDOW = 16, 2, 64, 128
MOBA_HEADS, MOBA_HEAD_DIM, MOBA_BLOCK, MOBA_TOPK = 8, 128, 256, 3
LN_EPS = 1e-5
RMS_EPS = 1e-6

IN_SPLITS = (
    GLA_HEADS * GLA_DK, GLA_HEADS * GLA_DK, GLA_HEADS * GLA_DV, GLA_RANK, BRANCH_WIDTH,
    SWA_Q_HEADS * SWA_HEAD_DIM, SWA_KV_HEADS * SWA_HEAD_DIM, SWA_KV_HEADS * SWA_HEAD_DIM, BRANCH_WIDTH,
    MOBA_HEADS * MOBA_HEAD_DIM, MOBA_HEADS * MOBA_HEAD_DIM, MOBA_HEADS * MOBA_HEAD_DIM, BRANCH_WIDTH,
    N_BRANCH * D_MODEL,
)
(SEG_AQ, SEG_AK, SEG_AV, SEG_ALR, SEG_AGATE, SEG_BQ, SEG_BK, SEG_BV, SEG_BGATE,
 SEG_CQ, SEG_CK, SEG_CV, SEG_CGATE, SEG_MGATE) = range(14)

LANES = 128
V7X_VMEM_BYTES = 64 * 1024 * 1024

PROJ_ORDER = (SEG_MGATE, SEG_AV, SEG_AGATE, SEG_BQ, SEG_BGATE, SEG_CQ, SEG_CK, SEG_CV, SEG_CGATE,
              SEG_AQ, SEG_AK, SEG_BK, SEG_BV, SEG_ALR)
PROJ_TILE_N = 512


def _proj_layout():
    src = np.concatenate([[0], np.cumsum(IN_SPLITS)])
    offs, pieces, cur = {}, [], 0
    for seg in PROJ_ORDER:
        width = IN_SPLITS[seg]
        padded = -(-width // LANES) * LANES
        offs[seg] = cur
        pieces.append((int(src[seg]), width, padded - width))
        cur += padded
    total = -(-cur // PROJ_TILE_N) * PROJ_TILE_N
    return offs, pieces, cur, total


PROJ_OFF, PROJ_PIECES, PROJ_USED, PROJ_COLS = _proj_layout()


PERMUTE_TILE_LANES = 128
PERMUTE_CHUNK_ROWS = 512


def _permute_kernel(wt_ref, o_ref):
    dst = 0
    for src, width, pad in PROJ_PIECES:
        for c0 in range(0, width, PERMUTE_CHUNK_ROWS):
            w = min(PERMUTE_CHUNK_ROWS, width - c0)
            o_ref[dst + c0:dst + c0 + w, :] = wt_ref[src + c0:src + c0 + w, :].astype(o_ref.dtype)
        if pad:
            o_ref[dst + width:dst + width + pad, :] = jnp.zeros((pad, o_ref.shape[1]), o_ref.dtype)
        dst += width + pad
    if PROJ_COLS > dst:
        o_ref[dst:, :] = jnp.zeros((PROJ_COLS - dst, o_ref.shape[1]), o_ref.dtype)


def _permute_w_in(w_in):
    depth, d, d_in = w_in.shape
    tl = PERMUTE_TILE_LANES
    vmem = 2 * (d_in * tl * 4 + PROJ_COLS * tl * 2) + (8 << 20)
    return pl.pallas_call(
        _permute_kernel,
        grid=(depth, d // tl),
        in_specs=[pl.BlockSpec((None, d_in, tl), lambda l, i: (l, 0, i))],
        out_specs=pl.BlockSpec((None, PROJ_COLS, tl), lambda l, i: (l, 0, i)),
        out_shape=jax.ShapeDtypeStruct((depth, PROJ_COLS, d), BF16),
        compiler_params=pltpu.CompilerParams(
            dimension_semantics=("parallel", "parallel"), vmem_limit_bytes=vmem),
        name="permute_w_in",
    )(jnp.swapaxes(w_in, 1, 2))


def _silu(x):
    return x * (1.0 / (1.0 + jnp.exp(-x)))


def _sigmoid(x):
    return 1.0 / (1.0 + jnp.exp(-x))


def _dot_nt(a, b):
    return lax.dot_general(a, b, (((1,), (1,)), ((), ())), preferred_element_type=F32)


PROJ_TILE_M = 2048


def _proj_kernel(x_ref, wt_ref, o_ref):
    o_ref[...] = _dot_nt(x_ref[...], wt_ref[...]).astype(o_ref.dtype)


def _proj(xb, wt, layer):
    m, d = xb.shape
    n = wt.shape[1]
    tm, tn = min(PROJ_TILE_M, m), PROJ_TILE_N
    vmem = 2 * (tm * d * 2 + d * tn * 2 + tm * tn * 2) + (8 << 20)
    return pl.pallas_call(
        _proj_kernel,
        grid=(m // tm, n // tn),
        in_specs=[pl.BlockSpec((tm, d), lambda i, j: (i, 0)),
                  pl.BlockSpec((None, tn, d), lambda i, j: (layer, j, 0))],
        out_specs=pl.BlockSpec((tm, tn), lambda i, j: (i, j)),
        out_shape=jax.ShapeDtypeStruct((m, n), BF16),
        compiler_params=pltpu.CompilerParams(
            dimension_semantics=("parallel", "arbitrary"), vmem_limit_bytes=vmem),
        name="in_proj",
    )(xb, wt)


GLA_CHUNK_ROWS = 128
GLA_SUB = 16
GLA_FAST_MAX_DECAY = 30.0
GLA_FAST_MIN_Q = 1e-20
GLA_FAST_MAX_QK = 1e20


def _gla_kernel(q_ref, k_ref, v_ref, gate_ref, alr_ref, wup_ref, bias_ref, ng_ref, o_ref,
                s_ref, b_ref, qs_ref, qx_ref, kf_ref, vf_ref, acc_ref):
    C, H, dk, dv = GLA_CHUNK_ROWS, GLA_HEADS, GLA_DK, GLA_DV

    @pl.when(pl.program_id(1) == 0)
    def _():
        s_ref[...] = jnp.zeros_like(s_ref)

    z = jnp.dot(alr_ref[...], wup_ref[...], preferred_element_type=F32) + bias_ref[...]
    g = -(jnp.maximum(-z, 0.0) + jnp.log1p(jnp.exp(-jnp.abs(z)))) * (1.0 / GLA_TAU)

    row = lax.broadcasted_iota(jnp.int32, (C, C), 0)
    col = lax.broadcasted_iota(jnp.int32, (C, C), 1)
    tril = jnp.where(col <= row, 1.0, 0.0).astype(BF16)
    g1 = g.astype(BF16)
    r1 = g - g1.astype(F32)
    g2 = r1.astype(BF16)
    g3 = (r1 - g2.astype(F32)).astype(BF16)
    b_all = (jnp.dot(tril, g1, preferred_element_type=F32)
             + jnp.dot(tril, g2, preferred_element_type=F32)
             + jnp.dot(tril, g3, preferred_element_type=F32))

    q_max = jnp.float32(0.0)
    k_max = jnp.float32(0.0)
    for h in range(H):
        b = b_all[:, h * dk:(h + 1) * dk]
        qs = q_ref[:, h * dk:(h + 1) * dk].astype(F32) * (dk ** -0.5)
        kf = k_ref[:, h * dk:(h + 1) * dk].astype(F32)
        qx = (qs * jnp.exp(b)).astype(BF16)
        b_ref[h] = b
        qs_ref[h] = qs
        kf_ref[h] = kf
        qx_ref[h] = qx
        q_max = jnp.maximum(q_max, jnp.max(jnp.abs(qs)))
        k_max = jnp.maximum(k_max, jnp.max(jnp.abs(kf)))
        acc_ref[h] = jnp.dot(qx, s_ref[h].astype(BF16), preferred_element_type=F32)

    total_decay = jnp.max(-b_all[C - 1:C, :])
    in_range = ((total_decay <= GLA_FAST_MAX_DECAY) & (q_max >= GLA_FAST_MIN_Q)
                & (q_max <= GLA_FAST_MAX_QK) & (k_max <= GLA_FAST_MAX_QK))

    @pl.when(in_range)
    def _():
        for h in range(H):
            k_inv = (kf_ref[h] * jnp.exp(-b_ref[h])).astype(BF16)
            a = jnp.where(col <= row, _dot_nt(qx_ref[h], k_inv), 0.0)
            acc_ref[h] += jnp.dot(a.astype(BF16), v_ref[:, h * dv:(h + 1) * dv],
                                  preferred_element_type=F32)

    @pl.when(jnp.logical_not(in_range))
    def _():
        _gla_sub_blocks(v_ref, b_ref, qs_ref, kf_ref, vf_ref, acc_ref)

    for h in range(H):
        b = b_ref[h]
        b_end = b[C - 1:C, :]
        k_end = kf_ref[h] * jnp.exp(b_end - b)
        decay_col = jnp.transpose(jnp.broadcast_to(jnp.exp(b_end), (dk, dk)))
        decay = jnp.concatenate([decay_col] * (dv // dk), axis=1)
        s_ref[h] = s_ref[h] * decay + jnp.dot(jnp.transpose(k_end).astype(BF16),
                                              v_ref[:, h * dv:(h + 1) * dv],
                                              preferred_element_type=F32)
        o = acc_ref[h]
        o = o * lax.rsqrt(jnp.mean(o * o, axis=-1, keepdims=True) + RMS_EPS) * ng_ref[...]
        gate = gate_ref[:, h * dv:(h + 1) * dv].astype(F32)
        o_ref[:, h * dv:(h + 1) * dv] = (o * _silu(gate)).astype(o_ref.dtype)


def _gla_sub_blocks(v_ref, b_ref, qs_ref, kf_ref, vf_ref, acc_ref):
    C, H, dk, dv = GLA_CHUNK_ROWS, GLA_HEADS, GLA_DK, GLA_DV
    for h in range(H):
        vf_ref[h] = v_ref[:, h * dv:(h + 1) * dv].astype(F32)
    key_idx = lax.broadcasted_iota(jnp.int32, (GLA_SUB, C), 1)
    sub_row = lax.broadcasted_iota(jnp.int32, (GLA_SUB, dk), 0)

    def sub_block(i, carry):
        base = pl.multiple_of(i * GLA_SUB, GLA_SUB)
        prev = jnp.maximum(base - 1, 0)
        for h in range(H):
            b_start = b_ref[h, pl.ds(prev, 1), :]
            b_i = b_ref[h, pl.ds(base, GLA_SUB), :]
            qs_i = qs_ref[h, pl.ds(base, GLA_SUB), :]
            kx = kf_ref[h] * jnp.exp(jnp.minimum(b_start - b_ref[h], 0.0))
            qx = qs_i * jnp.exp(jnp.minimum(b_i - b_start, 0.0))
            s = _dot_nt(qx.astype(BF16), kx.astype(BF16))
            s = jnp.where(key_idx < base, s, 0.0)
            o_past = jnp.dot(s.astype(BF16), v_ref[:, h * dv:(h + 1) * dv], preferred_element_type=F32)
            terms = []
            for j in range(GLA_SUB):
                b_j = b_ref[h, pl.ds(base + j, 1), :]
                k_j = kf_ref[h, pl.ds(base + j, 1), :]
                v_j = vf_ref[h, pl.ds(base + j, 1), :]
                t = jnp.exp(jnp.minimum(b_i - b_j, 0.0)) * (qs_i * k_j)
                t = jnp.where(sub_row >= j, t, 0.0)
                terms.append(jnp.sum(t, axis=-1, keepdims=True) * v_j)
            while len(terms) > 1:
                terms = [a + b for a, b in zip(terms[0::2], terms[1::2])]
            acc_ref[h, pl.ds(base, GLA_SUB), :] += terms[0] + o_past
        return carry

    lax.fori_loop(0, C // GLA_SUB, sub_block, 0)


def _gla(p, wup, bias, ng, batch, seq, layer):
    C, H, dk, dv = GLA_CHUNK_ROWS, GLA_HEADS, GLA_DK, GLA_DV
    nc = seq // C
    rows = lambda b, c: b * nc + c
    q0, k0 = PROJ_OFF[SEG_AQ] // (H * dk), PROJ_OFF[SEG_AK] // (H * dk)
    v0, g0 = PROJ_OFF[SEG_AV] // (H * dv), PROJ_OFF[SEG_AGATE] // (H * dv)
    lr0 = PROJ_OFF[SEG_ALR] // LANES
    return pl.pallas_call(
        _gla_kernel,
        grid=(batch, nc),
        in_specs=[
            pl.BlockSpec((C, H * dk), lambda b, c: (rows(b, c), q0)),
            pl.BlockSpec((C, H * dk), lambda b, c: (rows(b, c), k0)),
            pl.BlockSpec((C, H * dv), lambda b, c: (rows(b, c), v0)),
            pl.BlockSpec((C, H * dv), lambda b, c: (rows(b, c), g0)),
            pl.BlockSpec((C, LANES), lambda b, c: (rows(b, c), lr0)),
            pl.BlockSpec((None, LANES, H * dk), lambda b, c: (layer, 0, 0)),
            pl.BlockSpec((None, 1, H * dk), lambda b, c: (layer, 0, 0)),
            pl.BlockSpec((None, 1, dv), lambda b, c: (layer, 0, 0)),
        ],
        out_specs=pl.BlockSpec((C, H * dv), lambda b, c: (rows(b, c), 0)),
        out_shape=jax.ShapeDtypeStruct((batch * seq, H * dv), BF16),
        scratch_shapes=[
            pltpu.VMEM((H, dk, dv), F32),
            pltpu.VMEM((H, C, dk), F32),
            pltpu.VMEM((H, C, dk), F32),
            pltpu.VMEM((H, C, dk), BF16),
            pltpu.VMEM((H, C, dk), F32),
            pltpu.VMEM((H, C, dv), F32),
            pltpu.VMEM((H, C, dv), F32),
        ],
        compiler_params=pltpu.CompilerParams(dimension_semantics=("parallel", "arbitrary")),
        name="gla",
    )(p, p, p, p, p, wup, bias, ng)


def _swa_kernel(layer, sink_ref, q_ref, kp_ref, kc_ref, vp_ref, vc_ref, gate_ref, o_ref, s_ref, p_ref):
    W, hd = SWA_WINDOW, SWA_HEAD_DIM
    group = SWA_Q_HEADS // SWA_KV_HEADS
    pairs = group // 2
    NQ = group * W
    n = pl.program_id(1)
    kk = jnp.concatenate([kp_ref[...], kc_ref[...]], axis=0)
    vv = jnp.concatenate([vp_ref[...], vc_ref[...]], axis=0)
    vt_all = jnp.transpose(vv.astype(F32))
    key_lane = lax.broadcasted_iota(jnp.int32, (2 * W, LANES), 1)
    q_lane = lax.broadcasted_iota(jnp.int32, (W, LANES), 1)
    key = lax.broadcasted_iota(jnp.int32, (2 * W, W), 0)
    qry = lax.broadcasted_iota(jnp.int32, (2 * W, W), 1)
    allowed = (key > qry) & (key <= qry + W) & ((key >= W) | (n > 0))
    head_of_col = lax.broadcasted_iota(jnp.int32, (1, NQ), 1) // W
    swap = lambda t: jnp.concatenate([t[:, hd:], t[:, :hd]], axis=1)
    low_half = lambda t, lane, first: (jnp.where(lane < hd, t, jnp.zeros_like(t)) if first
                                       else swap(jnp.where(lane >= hd, t, jnp.zeros_like(t))))
    c = (hd ** -0.5) * LOG2_E
    inv_scale = float(hd) ** 0.5
    for hk in range(SWA_KV_HEADS):
        k_lo = low_half(kk, key_lane, hk == 0)
        vt = vt_all[hk * hd:(hk + 1) * hd, :].astype(BF16)
        q_stack = jnp.concatenate(
            [low_half(q_ref[:, (hk * group + 2 * pr) * hd:(hk * group + 2 * pr + 2) * hd], q_lane, first)
             for pr in range(pairs) for first in (True, False)], axis=0)
        sink = jnp.zeros((1, NQ), F32)
        for g in range(group):
            sink = jnp.where(head_of_col == g, sink_ref[layer, hk * group + g] * inv_scale, sink)
        m = sink
        for r in range(2):
            rows = slice(r * W, (r + 1) * W)
            sc = _dot_nt(k_lo[rows], q_stack)
            for g in range(group):
                cols = slice(g * W, (g + 1) * W)
                sg = jnp.where(allowed[rows], sc[:, cols], -1e30)
                s_ref[hk, rows, cols] = sg
            m = jnp.maximum(m, jnp.max(s_ref[hk, rows, :], axis=0, keepdims=True))
        den = jnp.exp2((sink - m) * c)
        for r in range(2):
            rows = slice(r * W, (r + 1) * W)
            p = jnp.exp2((s_ref[hk, rows, :] - m) * c)
            den = den + jnp.sum(p, axis=0, keepdims=True)
            p_ref[hk, rows, :] = p.astype(BF16)
        o_t = jnp.dot(vt, p_ref[hk], preferred_element_type=F32) * (1.0 / den)
        for pr in range(pairs):
            col0 = (hk * group + 2 * pr) * hd
            o = jnp.transpose(jnp.concatenate([o_t[:, 2 * pr * W:(2 * pr + 1) * W],
                                               o_t[:, (2 * pr + 1) * W:(2 * pr + 2) * W]], axis=0))
            gate = gate_ref[:, col0:col0 + LANES].astype(F32)
            o_ref[:, col0:col0 + LANES] = (o * _silu(gate)).astype(o_ref.dtype)


def _swa(p, sinks, batch, seq, layer):
    W = SWA_WINDOW
    nb = seq // W
    width = SWA_Q_HEADS * SWA_HEAD_DIM
    q0, g0 = PROJ_OFF[SEG_BQ] // width, PROJ_OFF[SEG_BGATE] // width
    k0, v0 = PROJ_OFF[SEG_BK] // LANES, PROJ_OFF[SEG_BV] // LANES
    cur = lambda b, n: b * nb + n
    prev = lambda b, n: b * nb + jnp.maximum(n - 1, 0)
    return pl.pallas_call(
        functools.partial(_swa_kernel, layer),
        grid=(batch, nb),
        in_specs=[
            pl.BlockSpec(memory_space=pltpu.SMEM),
            pl.BlockSpec((W, width), lambda b, n: (cur(b, n), q0)),
            pl.BlockSpec((W, LANES), lambda b, n: (prev(b, n), k0)),
            pl.BlockSpec((W, LANES), lambda b, n: (cur(b, n), k0)),
            pl.BlockSpec((W, LANES), lambda b, n: (prev(b, n), v0)),
            pl.BlockSpec((W, LANES), lambda b, n: (cur(b, n), v0)),
            pl.BlockSpec((W, width), lambda b, n: (cur(b, n), g0)),
        ],
        out_specs=pl.BlockSpec((W, width), lambda b, n: (cur(b, n), 0)),
        out_shape=jax.ShapeDtypeStruct((batch * seq, width), BF16),
        scratch_shapes=[
            pltpu.VMEM((SWA_KV_HEADS, 2 * W, SWA_Q_HEADS // SWA_KV_HEADS * W), F32),
            pltpu.VMEM((SWA_KV_HEADS, 2 * W, SWA_Q_HEADS // SWA_KV_HEADS * W), BF16),
        ],
        compiler_params=pltpu.CompilerParams(dimension_semantics=("parallel", "parallel")),
        name="swa",
    )(sinks, p, p, p, p, p, p)


MOBA_PENALTY = 1e30
LOG2_E = 1.4426950408889634


MOBA_HEAD_GROUP = 4


MOBA_KEY_CHUNK = 128


def _moba_kernel(q_ref, k_ref, v_ref, gate_ref, o_ref,
                 kmean_ref, vt_ref, pen_ref, acc_ref, s_ref, p_ref, so_ref, po_ref):
    BLK, hd, HG, RC = MOBA_BLOCK, MOBA_HEAD_DIM, MOBA_HEAD_GROUP, MOBA_KEY_CHUNK
    PAIR = 2 * BLK
    n_chunks = PAIR // RC
    i = pl.program_id(2)
    nblk = k_ref.shape[0] // BLK
    heads = [slice(h * hd, (h + 1) * hd) for h in range(HG)]
    c = (hd ** -0.5) * LOG2_E

    @pl.when(i == 0)
    def _():
        for h, cols in enumerate(heads):
            kf = k_ref[:, cols].astype(F32).reshape(nblk, BLK, hd)
            rest = jnp.sum(kf, axis=1) * (1.0 / BLK)
            rest = jnp.concatenate([rest, jnp.zeros((LANES - nblk, hd), F32)], axis=0)
            for t in range(3):
                term = rest.astype(BF16)
                kmean_ref[h, t * nblk:(t + 1) * nblk, :] = term[:nblk]
                rest = rest - term.astype(F32)
            for n in range(nblk):
                vt = jnp.transpose(v_ref[n * BLK:(n + 1) * BLK, cols].astype(F32)).astype(BF16)
                vt_ref[h, n // 2, :, (n % 2) * BLK:(n % 2 + 1) * BLK] = vt

    def score_chunk(h, pair, r):
        start = pl.multiple_of(pair * PAIR, PAIR)
        sc = _dot_nt(k_ref[pl.ds(start + r * RC, RC), heads[h]], q_ref[:, heads[h]])
        s_ref[h, r * RC:(r + 1) * RC, :] = sc
        return jnp.max(sc, axis=0, keepdims=True)

    def block_maxes(maxes):
        half = n_chunks // 2
        return functools.reduce(jnp.maximum, maxes[:half]), functools.reduce(jnp.maximum, maxes[half:])

    def softmax_step(h, pair, m_prev, l_prev, m_a, m_b, next_pair=None):
        pen_a = pen_ref[h, pl.ds(2 * pair, 1), :]
        pen_b = pen_ref[h, pl.ds(2 * pair + 1, 1), :]
        m_new = jnp.maximum(m_prev, jnp.maximum(m_a - pen_a, m_b - pen_b))
        alpha = jnp.exp2((m_prev - m_new) * c)
        l_new = alpha * l_prev
        maxes = []
        for r in range(n_chunks):
            off = m_new + (pen_a if r < n_chunks // 2 else pen_b)
            p = jnp.exp2((s_ref[h, r * RC:(r + 1) * RC, :] - off) * c)
            l_new = l_new + jnp.sum(p, axis=0, keepdims=True)
            p_ref[h, r * RC:(r + 1) * RC, :] = p.astype(BF16)
        if next_pair is not None:
            start = pl.multiple_of(next_pair * PAIR, PAIR)
            sc = _dot_nt(k_ref[pl.ds(start, PAIR), heads[h]], q_ref[:, heads[h]])
            s_ref[h] = sc
            maxes = [jnp.max(sc[:BLK], axis=0, keepdims=True), jnp.max(sc[BLK:], axis=0, keepdims=True)]
        acc_ref[h] = alpha * acc_ref[h] + jnp.dot(vt_ref[h, pair], p_ref[h], preferred_element_type=F32)
        if next_pair is None:
            return m_new, l_new
        return (m_new, l_new) + tuple(maxes)

    own = pl.multiple_of(i * BLK, BLK)
    key_c = lax.broadcasted_iota(jnp.int32, (RC, BLK), 0)
    qry_c = lax.broadcasted_iota(jnp.int32, (RC, BLK), 1)
    blk = lax.broadcasted_iota(jnp.int32, (nblk, BLK), 0)
    neg_inf = jnp.float32(-jnp.inf)

    carry = []
    for h, cols in enumerate(heads):
        q = q_ref[:, cols]
        lhs = jnp.concatenate([k_ref[pl.ds(own, BLK), cols], kmean_ref[h], k_ref[0:PAIR, cols]], axis=0)
        both = _dot_nt(lhs, q)
        gate = both[BLK:BLK + nblk] + both[BLK + nblk:BLK + 2 * nblk] + both[BLK + 2 * nblk:BLK + 3 * nblk]
        pair0 = both[BLK + 3 * nblk:]
        s_ref[h] = pair0
        pair0_max = (jnp.max(pair0[:BLK], axis=0, keepdims=True), jnp.max(pair0[BLK:], axis=0, keepdims=True))
        own_max = []
        for r in range(BLK // RC):
            rows = slice(r * RC, (r + 1) * RC)
            sc = jnp.where(key_c + r * RC <= qry_c, both[rows], -MOBA_PENALTY)
            so_ref[h, rows, :] = sc
            own_max.append(jnp.max(sc, axis=0, keepdims=True))

        g = jnp.where(blk < i, gate, neg_inf)
        sel = jnp.zeros(gate.shape, jnp.bool_)
        for _ in range(MOBA_TOPK):
            m = jnp.max(g, axis=0, keepdims=True)
            idx = jnp.min(jnp.where(g == m, blk, nblk), axis=0, keepdims=True)
            pick = (blk == idx) & (m > neg_inf)
            sel = sel | pick
            g = jnp.where(pick, neg_inf, g)
        pen_ref[h] = jnp.where(sel, 0.0, MOBA_PENALTY)

        m0 = functools.reduce(jnp.maximum, own_max)
        l0 = jnp.zeros_like(m0)
        for r in range(BLK // RC):
            rows = slice(r * RC, (r + 1) * RC)
            p = jnp.exp2((so_ref[h, rows, :] - m0) * c)
            l0 = l0 + jnp.sum(p, axis=0, keepdims=True)
            po_ref[h, rows, :] = p.astype(BF16)
        own_vt = vt_ref[h, i // 2, :, pl.ds(pl.multiple_of((i % 2) * BLK, BLK), BLK)]
        acc_ref[h] = jnp.dot(own_vt, po_ref[h], preferred_element_type=F32)
        carry.append((m0, l0) + pair0_max)

    n_pairs = (i + 1) // 2

    def pipelined(j, carry):
        return tuple(softmax_step(h, j, *carry[h], next_pair=j + 1) for h in range(HG))

    carry = lax.fori_loop(0, jnp.maximum(n_pairs - 1, 0), pipelined, tuple(carry))
    last = jnp.maximum(n_pairs - 1, 0)
    for h, cols in enumerate(heads):
        _, l_fin = softmax_step(h, last, *carry[h])
        o = jnp.transpose(acc_ref[h] * (1.0 / l_fin))
        o_ref[:, cols] = (o * _silu(gate_ref[:, cols].astype(F32))).astype(o_ref.dtype)


def _moba(p, batch, seq):
    BLK, hd, HG = MOBA_BLOCK, MOBA_HEAD_DIM, MOBA_HEAD_GROUP
    nblk = seq // BLK
    assert seq % (2 * BLK) == 0 and MOBA_HEADS % HG == 0
    gw = HG * hd
    q0, k0 = PROJ_OFF[SEG_CQ] // gw, PROJ_OFF[SEG_CK] // gw
    v0, g0 = PROJ_OFF[SEG_CV] // gw, PROJ_OFF[SEG_CGATE] // gw
    scratch_bytes = HG * (seq * hd * 2 + hd * BLK * 4 + 2 * BLK * BLK * 4 + 2 * BLK * BLK * 2)
    vmem = 2 * (2 * seq * gw * 2 + 3 * BLK * gw * 2) + scratch_bytes + (12 << 20)
    return pl.pallas_call(
        _moba_kernel,
        grid=(batch, MOBA_HEADS // HG, nblk),
        in_specs=[
            pl.BlockSpec((BLK, gw), lambda b, h, i: (b * nblk + i, q0 + h)),
            pl.BlockSpec((seq, gw), lambda b, h, i: (b, k0 + h)),
            pl.BlockSpec((seq, gw), lambda b, h, i: (b, v0 + h)),
            pl.BlockSpec((BLK, gw), lambda b, h, i: (b * nblk + i, g0 + h)),
        ],
        out_specs=pl.BlockSpec((BLK, gw), lambda b, h, i: (b * nblk + i, h)),
        out_shape=jax.ShapeDtypeStruct((batch * seq, MOBA_HEADS * hd), BF16),
        scratch_shapes=[
            pltpu.VMEM((HG, 3 * nblk, hd), BF16),
            pltpu.VMEM((HG, nblk // 2, hd, 2 * BLK), BF16),
            pltpu.VMEM((HG, nblk, BLK), F32),
            pltpu.VMEM((HG, hd, BLK), F32),
            pltpu.VMEM((HG, 2 * BLK, BLK), F32),
            pltpu.VMEM((HG, 2 * BLK, BLK), BF16),
            pltpu.VMEM((HG, BLK, BLK), F32),
            pltpu.VMEM((HG, BLK, BLK), BF16),
        ],
        compiler_params=pltpu.CompilerParams(
            dimension_semantics=("parallel", "parallel", "arbitrary"), vmem_limit_bytes=vmem),
        name="moba",
    )(p, p, p, p)


MERGE_TILE_M = 256


def _merge_kernel(alpha, ya_ref, yb_ref, yc_ref, mg_ref, x_ref, wb_ref, bm_ref, wo_ref, lg_ref, lb_ref,
                  y_ref, yb16_ref):
    D = D_MODEL
    merged = None
    for n, y_n in enumerate((ya_ref, yb_ref, yc_ref)):
        up = jnp.dot(y_n[...], wb_ref[n], preferred_element_type=F32)
        gate = _sigmoid(mg_ref[:, n * D:(n + 1) * D].astype(F32) + bm_ref[n:n + 1, :])
        merged = gate * up if merged is None else merged + gate * up
    out = jnp.dot(merged.astype(BF16), wo_ref[...], preferred_element_type=F32)
    h = alpha * x_ref[...] + out
    mu = jnp.mean(h, axis=-1, keepdims=True)
    hc = h - mu
    var = jnp.mean(hc * hc, axis=-1, keepdims=True)
    y = hc * lax.rsqrt(var + LN_EPS) * lg_ref[...] + lb_ref[...]
    y_ref[...] = y
    yb16_ref[...] = y.astype(BF16)


def _merge(ya, yb, yc, p, x, wb, bm, wo, lg, lb, alpha, layer):
    m = x.shape[0]
    D, Wd = D_MODEL, BRANCH_WIDTH
    tm = min(MERGE_TILE_M, m)
    once = pl.Buffered(1)
    resident = (N_BRANCH * Wd * D + D * D) * 2
    streamed = 2 * (3 * tm * Wd * 2 + tm * 3 * D * 2 + tm * D * 4 + tm * D * 4 + tm * D * 2)
    vmem = resident + streamed + (12 << 20)
    row = lambda i: (i, 0)
    return pl.pallas_call(
        functools.partial(_merge_kernel, alpha),
        grid=(m // tm,),
        in_specs=[
            pl.BlockSpec((tm, Wd), row), pl.BlockSpec((tm, Wd), row), pl.BlockSpec((tm, Wd), row),
            pl.BlockSpec((tm, N_BRANCH * D), lambda i: (i, PROJ_OFF[SEG_MGATE] // (N_BRANCH * D))),
            pl.BlockSpec((tm, D), row),
            pl.BlockSpec((None, N_BRANCH, Wd, D), lambda i: (layer, 0, 0, 0), pipeline_mode=once),
            pl.BlockSpec((None, N_BRANCH, D), lambda i: (layer, 0, 0)),
            pl.BlockSpec((None, D, D), lambda i: (layer, 0, 0), pipeline_mode=once),
            pl.BlockSpec((None, 1, D), lambda i: (layer, 0, 0)),
            pl.BlockSpec((None, 1, D), lambda i: (layer, 0, 0)),
        ],
        out_specs=[pl.BlockSpec((tm, D), row), pl.BlockSpec((tm, D), row)],
        out_shape=[jax.ShapeDtypeStruct((m, D), F32), jax.ShapeDtypeStruct((m, D), BF16)],
        compiler_params=pltpu.CompilerParams(
            dimension_semantics=("parallel",), vmem_limit_bytes=min(vmem, V7X_VMEM_BYTES - (4 << 20))),
        name="merge_out_ln",
    )(ya, yb, yc, p, x, wb, bm, wo, lg, lb)


def kernel(x, w_in, gla_w_up, gla_b, gla_norm_g, swa_sinks, b_merge, w_branch, w_o, ln_g, ln_b):
    batch, seq, d = x.shape
    depth = w_in.shape[0]
    alpha = (2 * depth) ** 0.25
    assert d == D_MODEL and seq % (2 * MOBA_BLOCK) == 0 and seq % GLA_CHUNK_ROWS == 0

    w_in_t = _permute_w_in(w_in)
    wup = jnp.pad(gla_w_up, ((0, 0), (0, LANES - GLA_RANK), (0, 0))).astype(BF16)
    wb = w_branch.astype(BF16)
    wo = w_o.astype(BF16)
    gla_bias, gla_ng = gla_b[:, None, :], gla_norm_g[:, None, :]
    lg, lb = ln_g[:, None, :], ln_b[:, None, :]

    xf = x.reshape(batch * seq, d)
    xb = xf.astype(BF16)
    for l in range(depth):
        p = _proj(xb, w_in_t, l)
        ya = _gla(p, wup, gla_bias, gla_ng, batch, seq, l)
        yb = _swa(p, swa_sinks, batch, seq, l)
        yc = _moba(p, batch, seq)
        xf, xb = _merge(ya, yb, yc, p, xf, wb, b_merge, wo, lg, lb, alpha, l)
    return xf.reshape(batch, seq, d)
```

```python
import functools

import jax
import jax.numpy as jnp
import numpy as np
from jax import lax
from jax.experimental import pallas as pl
from jax.experimental.pallas import tpu as pltpu

F32 = jnp.float32
BF16 = jnp.bfloat16

D_MODEL = 2048
BRANCH_WIDTH = 1024
N_BRANCH = 3
GLA_HEADS, GLA_DK, GLA_DV, GLA_RANK, GLA_TAU = 4, 128, 256, 16, 16.0
SWA_Q_HEADS, SWA_KV_HEADS, SWA_HEAD_DIM, SWA_WINDOW = 16, 2, 64, 128
MOBA_HEADS, MOBA_HEAD_DIM, MOBA_BLOCK, MOBA_TOPK = 8, 128, 256, 3
LN_EPS = 1e-5
RMS_EPS = 1e-6

IN_SPLITS = (
    GLA_HEADS * GLA_DK, GLA_HEADS * GLA_DK, GLA_HEADS * GLA_DV, GLA_RANK, BRANCH_WIDTH,
    SWA_Q_HEADS * SWA_HEAD_DIM, SWA_KV_HEADS * SWA_HEAD_DIM, SWA_KV_HEADS * SWA_HEAD_DIM, BRANCH_WIDTH,
    MOBA_HEADS * MOBA_HEAD_DIM, MOBA_HEADS * MOBA_HEAD_DIM, MOBA_HEADS * MOBA_HEAD_DIM, BRANCH_WIDTH,
    N_BRANCH * D_MODEL,
)
(SEG_AQ, SEG_AK, SEG_AV, SEG_ALR, SEG_AGATE, SEG_BQ, SEG_BK, SEG_BV, SEG_BGATE,
 SEG_CQ, SEG_CK, SEG_CV, SEG_CGATE, SEG_MGATE) = range(14)

LANES = 128
V7X_VMEM_BYTES = 64 * 1024 * 1024

PROJ_ORDER = (SEG_MGATE, SEG_AV, SEG_AGATE, SEG_BQ, SEG_BGATE, SEG_CQ, SEG_CK, SEG_CV, SEG_CGATE,
              SEG_AQ, SEG_AK)
PROJ_TILE_N = 512
TAIL_TILE_N = 256
SRC_OFF = [int(v) for v in np.concatenate([[0], np.cumsum(IN_SPLITS)])]


def _proj_layout():
    offs, tile_src, cur = {}, [], 0
    for seg in PROJ_ORDER:
        assert IN_SPLITS[seg] % PROJ_TILE_N == 0
        offs[seg] = cur
        tile_src += [SRC_OFF[seg] + t for t in range(0, IN_SPLITS[seg], PROJ_TILE_N)]
        cur += IN_SPLITS[seg]
    return offs, tile_src


PROJ_OFF, PROJ_TILE_SRC = _proj_layout()
assert SRC_OFF[SEG_BV] == SRC_OFF[SEG_BK] + IN_SPLITS[SEG_BK] and IN_SPLITS[SEG_BK] + IN_SPLITS[SEG_BV] == TAIL_TILE_N
TAIL_TILE_SRC = [SRC_OFF[SEG_BK], SRC_OFF[SEG_ALR]]
TAIL_OFF = {SEG_BK: 0, SEG_BV: IN_SPLITS[SEG_BK], SEG_ALR: TAIL_TILE_N}
PROJ_SRC_ALIGN = 16
assert all(v % PROJ_SRC_ALIGN == 0 for v in PROJ_TILE_SRC + TAIL_TILE_SRC)


def _silu(x):
    return x * (1.0 / (1.0 + jnp.exp(-x)))


def _sigmoid(x):
    return 1.0 / (1.0 + jnp.exp(-x))


def _dot_nt(a, b):
    return lax.dot_general(a, b, (((1,), (1,)), ((), ())), preferred_element_type=F32)


PROJ_TILE_M = 2048


def _proj_kernel(src_ref, x_ref, wt_ref, o_ref):
    del src_ref
    o_ref[...] = _dot_nt(x_ref[...], wt_ref[0].astype(BF16)).astype(o_ref.dtype)


def _proj(xb, wt, layer, tile_src, tn, name):
    m, d = xb.shape
    tm = min(PROJ_TILE_M, m)
    vmem = 2 * (tm * d * 2 + d * tn * 4 + tm * tn * 2) + d * tn * 2 + (8 << 20)
    grid_spec = pltpu.PrefetchScalarGridSpec(
        num_scalar_prefetch=1,
        grid=(m // tm, len(tile_src)),
        in_specs=[pl.BlockSpec((tm, d), lambda i, j, src: (i, 0)),
                  pl.BlockSpec((pl.Element(1), pl.Element(tn), pl.Element(d)),
                               lambda i, j, src: (layer, pl.multiple_of(src[j], PROJ_SRC_ALIGN), 0))],
        out_specs=pl.BlockSpec((tm, tn), lambda i, j, src: (i, j)),
    )
    return pl.pallas_call(
        _proj_kernel,
        grid_spec=grid_spec,
        out_shape=jax.ShapeDtypeStruct((m, len(tile_src) * tn), BF16),
        compiler_params=pltpu.CompilerParams(
            dimension_semantics=("parallel", "arbitrary"), vmem_limit_bytes=vmem),
        name=name,
    )(jnp.asarray(tile_src, jnp.int32), xb, wt)


GLA_CHUNK_ROWS = 128
GLA_SUB = 16
GLA_FAST_MAX_DECAY = 30.0
GLA_FAST_MIN_Q = 1e-20
GLA_FAST_MAX_QK = 1e20


def _gla_kernel(q_ref, k_ref, v_ref, gate_ref, alr_ref, wup_ref, bias_ref, ng_ref, o_ref,
                s_ref, b_ref, qs_ref, qx_ref, kf_ref, vf_ref, acc_ref, a_ref):
    C, H, dk, dv = GLA_CHUNK_ROWS, GLA_HEADS, GLA_DK, GLA_DV

    @pl.when(pl.program_id(1) == 0)
    def _():
        s_ref[...] = jnp.zeros_like(s_ref)

    rank_lane = lax.broadcasted_iota(jnp.int32, alr_ref.shape, 1) < GLA_RANK
    alr = jnp.where(rank_lane, alr_ref[...], jnp.zeros_like(alr_ref))
    z = jnp.dot(alr, wup_ref[...], preferred_element_type=F32) + bias_ref[...]
    g = -(jnp.maximum(-z, 0.0) + jnp.log1p(jnp.exp(-jnp.abs(z)))) * (1.0 / GLA_TAU)

    row = lax.broadcasted_iota(jnp.int32, (C, C), 0)
    col = lax.broadcasted_iota(jnp.int32, (C, C), 1)
    tril = jnp.where(col <= row, 1.0, 0.0).astype(BF16)
    g1 = g.astype(BF16)
    r1 = g - g1.astype(F32)
    g2 = r1.astype(BF16)
    g3 = (r1 - g2.astype(F32)).astype(BF16)
    b_all = (jnp.dot(tril, g1, preferred_element_type=F32)
             + jnp.dot(tril, g2, preferred_element_type=F32)
             + jnp.dot(tril, g3, preferred_element_type=F32))

    q_max = jnp.float32(0.0)
    k_max = jnp.float32(0.0)
    for h in range(H):
        b = b_all[:, h * dk:(h + 1) * dk]
        qs = q_ref[:, h * dk:(h + 1) * dk].astype(F32) * (dk ** -0.5)
        kf = k_ref[:, h * dk:(h + 1) * dk].astype(F32)
        qx = (qs * jnp.exp(b)).astype(BF16)
        b_ref[h] = b
        qs_ref[h] = qs
        kf_ref[h] = kf
        qx_ref[h] = qx
        q_max = jnp.maximum(q_max, jnp.max(jnp.abs(qs)))
        k_max = jnp.maximum(k_max, jnp.max(jnp.abs(kf)))
        acc_ref[h] = jnp.dot(qx, s_ref[h].astype(BF16), preferred_element_type=F32)

    total_decay = jnp.max(-b_all[C - 1:C, :])
    in_range = ((total_decay <= GLA_FAST_MAX_DECAY) & (q_max >= GLA_FAST_MIN_Q)
                & (q_max <= GLA_FAST_MAX_QK) & (k_max <= GLA_FAST_MAX_QK))

    @pl.when(in_range)
    def _():
        for h in range(H):
            k_inv = (kf_ref[h] * jnp.exp(-b_ref[h])).astype(BF16)
            a_ref[h] = jnp.where(col <= row, _dot_nt(qx_ref[h], k_inv), 0.0).astype(BF16)
        for h in range(H):
            acc_ref[h] += jnp.dot(a_ref[h], v_ref[:, h * dv:(h + 1) * dv], preferred_element_type=F32)

    @pl.when(jnp.logical_not(in_range))
    def _():
        _gla_sub_blocks(v_ref, b_ref, qs_ref, kf_ref, vf_ref, acc_ref)

    for h in range(H):
        b = b_ref[h]
        b_end = b[C - 1:C, :]
        k_end = kf_ref[h] * jnp.exp(b_end - b)
        decay_col = jnp.transpose(jnp.broadcast_to(jnp.exp(b_end), (dk, dk)))
        decay = jnp.concatenate([decay_col] * (dv // dk), axis=1)
        s_ref[h] = s_ref[h] * decay + jnp.dot(jnp.transpose(k_end).astype(BF16),
                                              v_ref[:, h * dv:(h + 1) * dv],
                                              preferred_element_type=F32)
        o = acc_ref[h]
        o = o * lax.rsqrt(jnp.mean(o * o, axis=-1, keepdims=True) + RMS_EPS) * ng_ref[...]
        gate = gate_ref[:, h * dv:(h + 1) * dv].astype(F32)
        o_ref[:, h * dv:(h + 1) * dv] = (o * _silu(gate)).astype(o_ref.dtype)


def _gla_sub_blocks(v_ref, b_ref, qs_ref, kf_ref, vf_ref, acc_ref):
    C, H, dk, dv = GLA_CHUNK_ROWS, GLA_HEADS, GLA_DK, GLA_DV
    for h in range(H):
        vf_ref[h] = v_ref[:, h * dv:(h + 1) * dv].astype(F32)
    key_idx = lax.broadcasted_iota(jnp.int32, (GLA_SUB, C), 1)
    sub_row = lax.broadcasted_iota(jnp.int32, (GLA_SUB, dk), 0)

    def sub_block(i, carry):
        base = pl.multiple_of(i * GLA_SUB, GLA_SUB)
        prev = jnp.maximum(base - 1, 0)
        for h in range(H):
            b_start = b_ref[h, pl.ds(prev, 1), :]
            b_i = b_ref[h, pl.ds(base, GLA_SUB), :]
            qs_i = qs_ref[h, pl.ds(base, GLA_SUB), :]
            kx = kf_ref[h] * jnp.exp(jnp.minimum(b_start - b_ref[h], 0.0))
            qx = qs_i * jnp.exp(jnp.minimum(b_i - b_start, 0.0))
            s = _dot_nt(qx.astype(BF16), kx.astype(BF16))
            s = jnp.where(key_idx < base, s, 0.0)
            o_past = jnp.dot(s.astype(BF16), v_ref[:, h * dv:(h + 1) * dv], preferred_element_type=F32)
            terms = []
            for j in range(GLA_SUB):
                b_j = b_ref[h, pl.ds(base + j, 1), :]
                k_j = kf_ref[h, pl.ds(base + j, 1), :]
                v_j = vf_ref[h, pl.ds(base + j, 1), :]
                t = jnp.exp(jnp.minimum(b_i - b_j, 0.0)) * (qs_i * k_j)
                t = jnp.where(sub_row >= j, t, 0.0)
                terms.append(jnp.sum(t, axis=-1, keepdims=True) * v_j)
            while len(terms) > 1:
                terms = [a + b for a, b in zip(terms[0::2], terms[1::2])]
            acc_ref[h, pl.ds(base, GLA_SUB), :] += terms[0] + o_past
        return carry

    lax.fori_loop(0, C // GLA_SUB, sub_block, 0)


def _gla(p, tail, wup, bias, ng, batch, seq, layer):
    C, H, dk, dv = GLA_CHUNK_ROWS, GLA_HEADS, GLA_DK, GLA_DV
    nc = seq // C
    rows = lambda b, c: b * nc + c
    q0, k0 = PROJ_OFF[SEG_AQ] // (H * dk), PROJ_OFF[SEG_AK] // (H * dk)
    v0, g0 = PROJ_OFF[SEG_AV] // (H * dv), PROJ_OFF[SEG_AGATE] // (H * dv)
    lr0 = TAIL_OFF[SEG_ALR] // LANES
    return pl.pallas_call(
        _gla_kernel,
        grid=(batch, nc),
        in_specs=[
            pl.BlockSpec((C, H * dk), lambda b, c: (rows(b, c), q0)),
            pl.BlockSpec((C, H * dk), lambda b, c: (rows(b, c), k0)),
            pl.BlockSpec((C, H * dv), lambda b, c: (rows(b, c), v0)),
            pl.BlockSpec((C, H * dv), lambda b, c: (rows(b, c), g0)),
            pl.BlockSpec((C, LANES), lambda b, c: (rows(b, c), lr0)),
            pl.BlockSpec((None, LANES, H * dk), lambda b, c: (layer, 0, 0)),
            pl.BlockSpec((None, 1, H * dk), lambda b, c: (layer, 0, 0)),
            pl.BlockSpec((None, 1, dv), lambda b, c: (layer, 0, 0)),
        ],
        out_specs=pl.BlockSpec((C, H * dv), lambda b, c: (rows(b, c), 0)),
        out_shape=jax.ShapeDtypeStruct((batch * seq, H * dv), BF16),
        scratch_shapes=[
            pltpu.VMEM((H, dk, dv), F32),
            pltpu.VMEM((H, C, dk), F32),
            pltpu.VMEM((H, C, dk), F32),
            pltpu.VMEM((H, C, dk), BF16),
            pltpu.VMEM((H, C, dk), F32),
            pltpu.VMEM((H, C, dv), F32),
            pltpu.VMEM((H, C, dv), F32),
            pltpu.VMEM((H, C, C), BF16),
        ],
        compiler_params=pltpu.CompilerParams(dimension_semantics=("parallel", "arbitrary")),
        name="gla",
    )(p, p, p, p, tail, wup, bias, ng)


def _swa_kernel(layer, sink_ref, q_ref, kp_ref, kc_ref, vp_ref, vc_ref, gate_ref, o_ref, s_ref, p_ref):
    W, hd = SWA_WINDOW, SWA_HEAD_DIM
    group = SWA_Q_HEADS // SWA_KV_HEADS
    pairs = group // 2
    NQ = group * W
    n = pl.program_id(1)
    kk = jnp.concatenate([kp_ref[...], kc_ref[...]], axis=0)
    vv = jnp.concatenate([vp_ref[...], vc_ref[...]], axis=0)
    vt_all = jnp.transpose(vv.astype(F32))
    key_lane = lax.broadcasted_iota(jnp.int32, (2 * W, LANES), 1)
    q_lane = lax.broadcasted_iota(jnp.int32, (W, LANES), 1)
    key = lax.broadcasted_iota(jnp.int32, (2 * W, W), 0)
    qry = lax.broadcasted_iota(jnp.int32, (2 * W, W), 1)
    allowed = (key > qry) & (key <= qry + W) & ((key >= W) | (n > 0))
    head_of_col = lax.broadcasted_iota(jnp.int32, (1, NQ), 1) // W
    swap = lambda t: jnp.concatenate([t[:, hd:], t[:, :hd]], axis=1)
    low_half = lambda t, lane, first: (jnp.where(lane < hd, t, jnp.zeros_like(t)) if first
                                       else swap(jnp.where(lane >= hd, t, jnp.zeros_like(t))))
    c = (hd ** -0.5) * LOG2_E
    inv_scale = float(hd) ** 0.5
    for hk in range(SWA_KV_HEADS):
        k_lo = low_half(kk, key_lane, hk == 0)
        vt = vt_all[hk * hd:(hk + 1) * hd, :].astype(BF16)
        q_stack = jnp.concatenate(
            [low_half(q_ref[:, (hk * group + 2 * pr) * hd:(hk * group + 2 * pr + 2) * hd], q_lane, first)
             for pr in range(pairs) for first in (True, False)], axis=0)
        sink = jnp.zeros((1, NQ), F32)
        for g in range(group):
            sink = jnp.where(head_of_col == g, sink_ref[layer, hk * group + g] * inv_scale, sink)
        m = sink
        for r in range(2):
            rows = slice(r * W, (r + 1) * W)
            sc = _dot_nt(k_lo[rows], q_stack)
            for g in range(group):
                cols = slice(g * W, (g + 1) * W)
                sg = jnp.where(allowed[rows], sc[:, cols], -1e30)
                s_ref[hk, rows, cols] = sg
            m = jnp.maximum(m, jnp.max(s_ref[hk, rows, :], axis=0, keepdims=True))
        den = jnp.exp2((sink - m) * c)
        for r in range(2):
            rows = slice(r * W, (r + 1) * W)
            p = jnp.exp2((s_ref[hk, rows, :] - m) * c)
            den = den + jnp.sum(p, axis=0, keepdims=True)
            p_ref[hk, rows, :] = p.astype(BF16)
        o_t = jnp.dot(vt, p_ref[hk], preferred_element_type=F32) * (1.0 / den)
        for pr in range(pairs):
            col0 = (hk * group + 2 * pr) * hd
            o = jnp.transpose(jnp.concatenate([o_t[:, 2 * pr * W:(2 * pr + 1) * W],
                                               o_t[:, (2 * pr + 1) * W:(2 * pr + 2) * W]], axis=0))
            gate = gate_ref[:, col0:col0 + LANES].astype(F32)
            o_ref[:, col0:col0 + LANES] = (o * _silu(gate)).astype(o_ref.dtype)


def _swa(p, tail, sinks, batch, seq, layer):
    W = SWA_WINDOW
    nb = seq // W
    width = SWA_Q_HEADS * SWA_HEAD_DIM
    q0, g0 = PROJ_OFF[SEG_BQ] // width, PROJ_OFF[SEG_BGATE] // width
    k0, v0 = TAIL_OFF[SEG_BK] // LANES, TAIL_OFF[SEG_BV] // LANES
    cur = lambda b, n: b * nb + n
    prev = lambda b, n: b * nb + jnp.maximum(n - 1, 0)
    return pl.pallas_call(
        functools.partial(_swa_kernel, layer),
        grid=(batch, nb),
        in_specs=[
            pl.BlockSpec(memory_space=pltpu.SMEM),
            pl.BlockSpec((W, width), lambda b, n: (cur(b, n), q0)),
            pl.BlockSpec((W, LANES), lambda b, n: (prev(b, n), k0)),
            pl.BlockSpec((W, LANES), lambda b, n: (cur(b, n), k0)),
            pl.BlockSpec((W, LANES), lambda b, n: (prev(b, n), v0)),
            pl.BlockSpec((W, LANES), lambda b, n: (cur(b, n), v0)),
            pl.BlockSpec((W, width), lambda b, n: (cur(b, n), g0)),
        ],
        out_specs=pl.BlockSpec((W, width), lambda b, n: (cur(b, n), 0)),
        out_shape=jax.ShapeDtypeStruct((batch * seq, width), BF16),
        scratch_shapes=[
            pltpu.VMEM((SWA_KV_HEADS, 2 * W, SWA_Q_HEADS // SWA_KV_HEADS * W), F32),
            pltpu.VMEM((SWA_KV_HEADS, 2 * W, SWA_Q_HEADS // SWA_KV_HEADS * W), BF16),
        ],
        compiler_params=pltpu.CompilerParams(dimension_semantics=("parallel", "parallel")),
        name="swa",
    )(sinks, p, tail, tail, tail, tail, p)


MOBA_PENALTY = 1e30
LOG2_E = 1.4426950408889634


MOBA_HEAD_GROUP = 4


MOBA_KEY_CHUNK = 128
MOBA_SUM_ROWS = 16


def _moba_kernel(q_ref, k_ref, v_ref, gate_ref, o_ref,
                 kmean_ref, vt_ref, pen_ref, acc_ref, s_ref, p_ref, so_ref, po_ref):
    BLK, hd, HG, RC = MOBA_BLOCK, MOBA_HEAD_DIM, MOBA_HEAD_GROUP, MOBA_KEY_CHUNK
    PAIR = 2 * BLK
    n_chunks = PAIR // RC
    i = pl.program_id(2)
    nblk = k_ref.shape[0] // BLK
    heads = [slice(h * hd, (h + 1) * hd) for h in range(HG)]
    c = (hd ** -0.5) * LOG2_E

    @pl.when(i == 0)
    def _():
        for h, cols in enumerate(heads):
            kf = k_ref[:, cols].astype(F32).reshape(nblk, BLK, hd)
            rest = jnp.sum(kf, axis=1) * (1.0 / BLK)
            rest = jnp.concatenate([rest, jnp.zeros((LANES - nblk, hd), F32)], axis=0)
            for t in range(3):
                term = rest.astype(BF16)
                kmean_ref[h, t * nblk:(t + 1) * nblk, :] = term[:nblk]
                rest = rest - term.astype(F32)
            extra = lax.broadcasted_iota(jnp.int32, (MOBA_SUM_ROWS, BLK), 0)
            ones_rows = jnp.where(extra == 0, 1.0, 0.0).astype(BF16)
            for n in range(nblk):
                vt = jnp.transpose(v_ref[n * BLK:(n + 1) * BLK, cols].astype(F32)).astype(BF16)
                vt_ref[h, n // 2, :, (n % 2) * BLK:(n % 2 + 1) * BLK] = jnp.concatenate([vt, ones_rows], axis=0)

    def score_chunk(h, pair, r):
        start = pl.multiple_of(pair * PAIR, PAIR)
        sc = _dot_nt(k_ref[pl.ds(start + r * RC, RC), heads[h]], q_ref[:, heads[h]])
        s_ref[h, r * RC:(r + 1) * RC, :] = sc
        return jnp.max(sc, axis=0, keepdims=True)

    def block_maxes(maxes):
        half = n_chunks // 2
        return functools.reduce(jnp.maximum, maxes[:half]), functools.reduce(jnp.maximum, maxes[half:])

    def softmax_step(h, pair, m_prev, m_a, m_b, next_pair=None):
        pen_a = pen_ref[h, pl.ds(2 * pair, 1), :]
        pen_b = pen_ref[h, pl.ds(2 * pair + 1, 1), :]
        m_new = jnp.maximum(m_prev, jnp.maximum(m_a - pen_a, m_b - pen_b))
        alpha = jnp.exp2((m_prev - m_new) * c)
        maxes = []
        for r in range(n_chunks):
            off = m_new + (pen_a if r < n_chunks // 2 else pen_b)
            p = jnp.exp2((s_ref[h, r * RC:(r + 1) * RC, :] - off) * c)
            p_ref[h, r * RC:(r + 1) * RC, :] = p.astype(BF16)
        if next_pair is not None:
            start = pl.multiple_of(next_pair * PAIR, PAIR)
            sc = _dot_nt(k_ref[pl.ds(start, PAIR), heads[h]], q_ref[:, heads[h]])
            s_ref[h] = sc
            maxes = [jnp.max(sc[:BLK], axis=0, keepdims=True), jnp.max(sc[BLK:], axis=0, keepdims=True)]
        acc_ref[h] = alpha * acc_ref[h] + jnp.dot(vt_ref[h, pair], p_ref[h], preferred_element_type=F32)
        if next_pair is None:
            return (m_new,)
        return (m_new,) + tuple(maxes)

    own = pl.multiple_of(i * BLK, BLK)
    key_c = lax.broadcasted_iota(jnp.int32, (RC, BLK), 0)
    qry_c = lax.broadcasted_iota(jnp.int32, (RC, BLK), 1)
    blk = lax.broadcasted_iota(jnp.int32, (nblk, BLK), 0)
    neg_inf = jnp.float32(-jnp.inf)

    carry = []
    for h, cols in enumerate(heads):
        q = q_ref[:, cols]
        lhs = jnp.concatenate([k_ref[pl.ds(own, BLK), cols], kmean_ref[h], k_ref[0:PAIR, cols]], axis=0)
        both = _dot_nt(lhs, q)
        gate = both[BLK:BLK + nblk] + both[BLK + nblk:BLK + 2 * nblk] + both[BLK + 2 * nblk:BLK + 3 * nblk]
        pair0 = both[BLK + 3 * nblk:]
        s_ref[h] = pair0
        pair0_max = (jnp.max(pair0[:BLK], axis=0, keepdims=True), jnp.max(pair0[BLK:], axis=0, keepdims=True))
        own_max = []
        for r in range(BLK // RC):
            rows = slice(r * RC, (r + 1) * RC)
            sc = jnp.where(key_c + r * RC <= qry_c, both[rows], -MOBA_PENALTY)
            so_ref[h, rows, :] = sc
            own_max.append(jnp.max(sc, axis=0, keepdims=True))

        g = jnp.where(blk < i, gate, neg_inf)
        sel = jnp.zeros(gate.shape, jnp.bool_)
        for _ in range(MOBA_TOPK):
            m = jnp.max(g, axis=0, keepdims=True)
            idx = jnp.min(jnp.where(g == m, blk, nblk), axis=0, keepdims=True)
            pick = (blk == idx) & (m > neg_inf)
            sel = sel | pick
            g = jnp.where(pick, neg_inf, g)
        pen_ref[h] = jnp.where(sel, 0.0, MOBA_PENALTY)

        m0 = functools.reduce(jnp.maximum, own_max)
        for r in range(BLK // RC):
            rows = slice(r * RC, (r + 1) * RC)
            po_ref[h, rows, :] = jnp.exp2((so_ref[h, rows, :] - m0) * c).astype(BF16)
        own_vt = vt_ref[h, i // 2, :, pl.ds(pl.multiple_of((i % 2) * BLK, BLK), BLK)]
        acc_ref[h] = jnp.dot(own_vt, po_ref[h], preferred_element_type=F32)
        carry.append((m0,) + pair0_max)

    n_pairs = (i + 1) // 2

    def pipelined(j, carry):
        return tuple(softmax_step(h, j, *carry[h], next_pair=j + 1) for h in range(HG))

    carry = lax.fori_loop(0, jnp.maximum(n_pairs - 1, 0), pipelined, tuple(carry))
    last = jnp.maximum(n_pairs - 1, 0)
    for h, cols in enumerate(heads):
        softmax_step(h, last, *carry[h])
        o = jnp.transpose(acc_ref[h, 0:hd, :] * (1.0 / acc_ref[h, hd:hd + 1, :]))
        o_ref[:, cols] = (o * _silu(gate_ref[:, cols].astype(F32))).astype(o_ref.dtype)


def _moba(p, batch, seq):
    BLK, hd, HG = MOBA_BLOCK, MOBA_HEAD_DIM, MOBA_HEAD_GROUP
    nblk = seq // BLK
    assert seq % (2 * BLK) == 0 and MOBA_HEADS % HG == 0
    gw = HG * hd
    q0, k0 = PROJ_OFF[SEG_CQ] // gw, PROJ_OFF[SEG_CK] // gw
    v0, g0 = PROJ_OFF[SEG_CV] // gw, PROJ_OFF[SEG_CGATE] // gw
    scratch_bytes = HG * (seq * hd * 2 + hd * BLK * 4 + 2 * BLK * BLK * 4 + 2 * BLK * BLK * 2)
    vmem = 2 * (2 * seq * gw * 2 + 3 * BLK * gw * 2) + scratch_bytes + (12 << 20)
    return pl.pallas_call(
        _moba_kernel,
        grid=(batch, MOBA_HEADS // HG, nblk),
        in_specs=[
            pl.BlockSpec((BLK, gw), lambda b, h, i: (b * nblk + i, q0 + h)),
            pl.BlockSpec((seq, gw), lambda b, h, i: (b, k0 + h)),
            pl.BlockSpec((seq, gw), lambda b, h, i: (b, v0 + h)),
            pl.BlockSpec((BLK, gw), lambda b, h, i: (b * nblk + i, g0 + h)),
        ],
        out_specs=pl.BlockSpec((BLK, gw), lambda b, h, i: (b * nblk + i, h)),
        out_shape=jax.ShapeDtypeStruct((batch * seq, MOBA_HEADS * hd), BF16),
        scratch_shapes=[
            pltpu.VMEM((HG, 3 * nblk, hd), BF16),
            pltpu.VMEM((HG, nblk // 2, hd + MOBA_SUM_ROWS, 2 * BLK), BF16),
            pltpu.VMEM((HG, nblk, BLK), F32),
            pltpu.VMEM((HG, hd + MOBA_SUM_ROWS, BLK), F32),
            pltpu.VMEM((HG, 2 * BLK, BLK), F32),
            pltpu.VMEM((HG, 2 * BLK, BLK), BF16),
            pltpu.VMEM((HG, BLK, BLK), F32),
            pltpu.VMEM((HG, BLK, BLK), BF16),
        ],
        compiler_params=pltpu.CompilerParams(
            dimension_semantics=("parallel", "parallel", "arbitrary"), vmem_limit_bytes=vmem),
        name="moba",
    )(p, p, p, p)


MERGE_TILE_M = 256


def _merge_kernel(alpha, ya_ref, yb_ref, yc_ref, mg_ref, x_ref, wb_ref, bm_ref, wo_ref, lg_ref, lb_ref,
                  y_ref, yb16_ref):
    D = D_MODEL
    merged = None
    for n, y_n in enumerate((ya_ref, yb_ref, yc_ref)):
        up = jnp.dot(y_n[...], wb_ref[n], preferred_element_type=F32)
        gate = _sigmoid(mg_ref[:, n * D:(n + 1) * D].astype(F32) + bm_ref[n:n + 1, :])
        merged = gate * up if merged is None else merged + gate * up
    out = jnp.dot(merged.astype(BF16), wo_ref[...], preferred_element_type=F32)
    h = alpha * x_ref[...] + out
    mu = jnp.mean(h, axis=-1, keepdims=True)
    hc = h - mu
    var = jnp.mean(hc * hc, axis=-1, keepdims=True)
    y = hc * lax.rsqrt(var + LN_EPS) * lg_ref[...] + lb_ref[...]
    y_ref[...] = y
    yb16_ref[...] = y.astype(BF16)


def _merge(ya, yb, yc, p, x, wb, bm, wo, lg, lb, alpha, layer):
    m = x.shape[0]
    D, Wd = D_MODEL, BRANCH_WIDTH
    tm = min(MERGE_TILE_M, m)
    once = pl.Buffered(1)
    resident = (N_BRANCH * Wd * D + D * D) * 2
    streamed = 2 * (3 * tm * Wd * 2 + tm * 3 * D * 2 + tm * D * 4 + tm * D * 4 + tm * D * 2)
    vmem = resident + streamed + (12 << 20)
    row = lambda i: (i, 0)
    return pl.pallas_call(
        functools.partial(_merge_kernel, alpha),
        grid=(m // tm,),
        in_specs=[
            pl.BlockSpec((tm, Wd), row), pl.BlockSpec((tm, Wd), row), pl.BlockSpec((tm, Wd), row),
            pl.BlockSpec((tm, N_BRANCH * D), lambda i: (i, PROJ_OFF[SEG_MGATE] // (N_BRANCH * D))),
            pl.BlockSpec((tm, D), row),
            pl.BlockSpec((None, N_BRANCH, Wd, D), lambda i: (layer, 0, 0, 0), pipeline_mode=once),
            pl.BlockSpec((None, N_BRANCH, D), lambda i: (layer, 0, 0)),
            pl.BlockSpec((None, D, D), lambda i: (layer, 0, 0), pipeline_mode=once),
            pl.BlockSpec((None, 1, D), lambda i: (layer, 0, 0)),
            pl.BlockSpec((None, 1, D), lambda i: (layer, 0, 0)),
        ],
        out_specs=[pl.BlockSpec((tm, D), row), pl.BlockSpec((tm, D), row)],
        out_shape=[jax.ShapeDtypeStruct((m, D), F32), jax.ShapeDtypeStruct((m, D), BF16)],
        compiler_params=pltpu.CompilerParams(
            dimension_semantics=("parallel",), vmem_limit_bytes=min(vmem, V7X_VMEM_BYTES - (4 << 20))),
        name="merge_out_ln",
    )(ya, yb, yc, p, x, wb, bm, wo, lg, lb)


def kernel(x, w_in, gla_w_up, gla_b, gla_norm_g, swa_sinks, b_merge, w_branch, w_o, ln_g, ln_b):
    batch, seq, d = x.shape
    depth = w_in.shape[0]
    alpha = (2 * depth) ** 0.25
    assert d == D_MODEL and seq % (2 * MOBA_BLOCK) == 0 and seq % GLA_CHUNK_ROWS == 0

    w_in_t = jnp.swapaxes(w_in, 1, 2)
    wup = jnp.pad(gla_w_up, ((0, 0), (0, LANES - GLA_RANK), (0, 0))).astype(BF16)
    wb = w_branch.astype(BF16)
    wo = w_o.astype(BF16)
    gla_bias, gla_ng = gla_b[:, None, :], gla_norm_g[:, None, :]
    lg, lb = ln_g[:, None, :], ln_b[:, None, :]

    xf = x.reshape(batch * seq, d)
    xb = xf.astype(BF16)
    for l in range(depth):
        p = _proj(xb, w_in_t, l, PROJ_TILE_SRC, PROJ_TILE_N, "in_proj")
        tail = _proj(xb, w_in_t, l, TAIL_TILE_SRC, TAIL_TILE_N, "in_proj_tail")
        ya = _gla(p, tail, wup, gla_bias, gla_ng, batch, seq, l)
        yb = _swa(p, tail, swa_sinks, batch, seq, l)
        yc = _moba(p, batch, seq)
        xf, xb = _merge(ya, yb, yc, p, xf, wb, b_merge, wo, lg, lb, alpha, l)
    return xf.reshape(batch, seq, d)
```

```python
import functools

import jax
import jax.numpy as jnp
import numpy as np
from jax import lax
from jax.experimental import pallas as pl
from jax.experimental.pallas import tpu as pltpu

F32 = jnp.float32
BF16 = jnp.bfloat16

D_MODEL = 2048
BRANCH_WIDTH = 1024
N_BRANCH = 3
GLA_HEADS, GLA_DK, GLA_DV, GLA_RANK, GLA_TAU = 4, 128, 256, 16, 16.0
SWA_Q_HEADS, SWA_KV_HEADS, SWA_HEAD_DIM, SWA_WINDOW = 16, 2, 64, 128
MOBA_HEADS, MOBA_HEAD_DIM, MOBA_BLOCK, MOBA_TOPK = 8, 128, 256, 3
LN_EPS = 1e-5
RMS_EPS = 1e-6

IN_SPLITS = (
    GLA_HEADS * GLA_DK, GLA_HEADS * GLA_DK, GLA_HEADS * GLA_DV, GLA_RANK, BRANCH_WIDTH,
    SWA_Q_HEADS * SWA_HEAD_DIM, SWA_KV_HEADS * SWA_HEAD_DIM, SWA_KV_HEADS * SWA_HEAD_DIM, BRANCH_WIDTH,
    MOBA_HEADS * MOBA_HEAD_DIM, MOBA_HEADS * MOBA_HEAD_DIM, MOBA_HEADS * MOBA_HEAD_DIM, BRANCH_WIDTH,
    N_BRANCH * D_MODEL,
)
(SEG_AQ, SEG_AK, SEG_AV, SEG_ALR, SEG_AGATE, SEG_BQ, SEG_BK, SEG_BV, SEG_BGATE,
 SEG_CQ, SEG_CK, SEG_CV, SEG_CGATE, SEG_MGATE) = range(14)

LANES = 128
V7X_VMEM_BYTES = 64 * 1024 * 1024

PROJ_ORDER = (SEG_MGATE, SEG_AV, SEG_AGATE, SEG_BQ, SEG_BGATE, SEG_CQ, SEG_CK, SEG_CV, SEG_CGATE,
              SEG_AQ, SEG_AK)
PROJ_TILE_N = 1024
TAIL_TILE_N = 256
SRC_OFF = [int(v) for v in np.concatenate([[0], np.cumsum(IN_SPLITS)])]


def _proj_layout():
    offs, runs, cur = {}, [], 0
    for seg in PROJ_ORDER:
        offs[seg] = cur
        cur += IN_SPLITS[seg]
        if runs and runs[-1][0] + runs[-1][1] == SRC_OFF[seg]:
            runs[-1][1] += IN_SPLITS[seg]
        else:
            runs.append([SRC_OFF[seg], IN_SPLITS[seg]])
    tile_src = []
    for start, length in runs:
        assert length % PROJ_TILE_N == 0
        tile_src += [start + t for t in range(0, length, PROJ_TILE_N)]
    return offs, tile_src


PROJ_OFF, PROJ_TILE_SRC = _proj_layout()
assert SRC_OFF[SEG_BV] == SRC_OFF[SEG_BK] + IN_SPLITS[SEG_BK] and IN_SPLITS[SEG_BK] + IN_SPLITS[SEG_BV] == TAIL_TILE_N
TAIL_TILE_SRC = [SRC_OFF[SEG_BK], SRC_OFF[SEG_ALR]]
TAIL_OFF = {SEG_BK: 0, SEG_BV: IN_SPLITS[SEG_BK], SEG_ALR: TAIL_TILE_N}
PROJ_SRC_ALIGN = 16
assert all(v % PROJ_SRC_ALIGN == 0 for v in PROJ_TILE_SRC + TAIL_TILE_SRC)


def _silu(x):
    return x * (1.0 / (1.0 + jnp.exp(-x)))


def _sigmoid(x):
    return 1.0 / (1.0 + jnp.exp(-x))


def _dot_nt(a, b):
    return lax.dot_general(a, b, (((1,), (1,)), ((), ())), preferred_element_type=F32)


PROJ_TILE_M = 2048


def _proj_kernel(src_ref, x_ref, wt_ref, o_ref):
    del src_ref
    o_ref[...] = _dot_nt(x_ref[...], wt_ref[0].astype(BF16)).astype(o_ref.dtype)


def _proj(xb, wt, layer, tile_src, tn, name):
    m, d = xb.shape
    tm = min(PROJ_TILE_M, m)
    vmem = 2 * (tm * d * 2 + d * tn * 4 + tm * tn * 2) + d * tn * 2 + (8 << 20)
    grid_spec = pltpu.PrefetchScalarGridSpec(
        num_scalar_prefetch=1,
        grid=(m // tm, len(tile_src)),
        in_specs=[pl.BlockSpec((tm, d), lambda i, j, src: (i, 0)),
                  pl.BlockSpec((pl.Element(1), pl.Element(tn), pl.Element(d)),
                               lambda i, j, src: (layer, pl.multiple_of(src[j], PROJ_SRC_ALIGN), 0))],
        out_specs=pl.BlockSpec((tm, tn), lambda i, j, src: (i, j)),
    )
    return pl.pallas_call(
        _proj_kernel,
        grid_spec=grid_spec,
        out_shape=jax.ShapeDtypeStruct((m, len(tile_src) * tn), BF16),
        compiler_params=pltpu.CompilerParams(
            dimension_semantics=("parallel", "arbitrary"), vmem_limit_bytes=vmem),
        name=name,
    )(jnp.asarray(tile_src, jnp.int32), xb, wt)


GLA_CHUNK_ROWS = 128
GLA_SUB = 16
GLA_FAST_MAX_DECAY = 30.0
GLA_FAST_MIN_Q = 1e-20
GLA_FAST_MAX_QK = 1e20


def _gla_kernel(q_ref, k_ref, v_ref, gate_ref, alr_ref, wup_ref, bias_ref, ng_ref, o_ref,
                s_ref, b_ref, qs_ref, qx_ref, kf_ref, vf_ref, acc_ref, a_ref):
    C, H, dk, dv = GLA_CHUNK_ROWS, GLA_HEADS, GLA_DK, GLA_DV

    @pl.when(pl.program_id(1) == 0)
    def _():
        s_ref[...] = jnp.zeros_like(s_ref)

    rank_lane = lax.broadcasted_iota(jnp.int32, alr_ref.shape, 1) < GLA_RANK
    alr = jnp.where(rank_lane, alr_ref[...], jnp.zeros_like(alr_ref))
    z = jnp.dot(alr, wup_ref[...], preferred_element_type=F32) + bias_ref[...]
    g = -(jnp.maximum(-z, 0.0) + jnp.log1p(jnp.exp(-jnp.abs(z)))) * (1.0 / GLA_TAU)

    row = lax.broadcasted_iota(jnp.int32, (C, C), 0)
    col = lax.broadcasted_iota(jnp.int32, (C, C), 1)
    tril = jnp.where(col <= row, 1.0, 0.0).astype(BF16)
    g1 = g.astype(BF16)
    r1 = g - g1.astype(F32)
    g2 = r1.astype(BF16)
    g3 = (r1 - g2.astype(F32)).astype(BF16)
    b_all = (jnp.dot(tril, g1, preferred_element_type=F32)
             + jnp.dot(tril, g2, preferred_element_type=F32)
             + jnp.dot(tril, g3, preferred_element_type=F32))

    q_max = jnp.float32(0.0)
    k_max = jnp.float32(0.0)
    for h in range(H):
        b = b_all[:, h * dk:(h + 1) * dk]
        qs = q_ref[:, h * dk:(h + 1) * dk].astype(F32) * (dk ** -0.5)
        kf = k_ref[:, h * dk:(h + 1) * dk].astype(F32)
        qx = (qs * jnp.exp(b)).astype(BF16)
        b_ref[h] = b
        qs_ref[h] = qs
        kf_ref[h] = kf
        qx_ref[h] = qx
        q_max = jnp.maximum(q_max, jnp.max(jnp.abs(qs)))
        k_max = jnp.maximum(k_max, jnp.max(jnp.abs(kf)))
        acc_ref[h] = jnp.dot(qx, s_ref[h].astype(BF16), preferred_element_type=F32)

    total_decay = jnp.max(-b_all[C - 1:C, :])
    in_range = ((total_decay <= GLA_FAST_MAX_DECAY) & (q_max >= GLA_FAST_MIN_Q)
                & (q_max <= GLA_FAST_MAX_QK) & (k_max <= GLA_FAST_MAX_QK))

    @pl.when(in_range)
    def _():
        for h in range(H):
            k_inv = (kf_ref[h] * jnp.exp(-b_ref[h])).astype(BF16)
            a_ref[h] = jnp.where(col <= row, _dot_nt(qx_ref[h], k_inv), 0.0).astype(BF16)
        for h in range(H):
            acc_ref[h] += jnp.dot(a_ref[h], v_ref[:, h * dv:(h + 1) * dv], preferred_element_type=F32)

    @pl.when(jnp.logical_not(in_range))
    def _():
        _gla_sub_blocks(v_ref, b_ref, qs_ref, kf_ref, vf_ref, acc_ref)

    for h in range(H):
        b = b_ref[h]
        b_end = b[C - 1:C, :]
        k_end = kf_ref[h] * jnp.exp(b_end - b)
        decay_col = jnp.transpose(jnp.broadcast_to(jnp.exp(b_end), (dk, dk)))
        decay = jnp.concatenate([decay_col] * (dv // dk), axis=1)
        s_ref[h] = s_ref[h] * decay + jnp.dot(jnp.transpose(k_end).astype(BF16),
                                              v_ref[:, h * dv:(h + 1) * dv],
                                              preferred_element_type=F32)
        o = acc_ref[h]
        o = o * lax.rsqrt(jnp.mean(o * o, axis=-1, keepdims=True) + RMS_EPS) * ng_ref[...]
        gate = gate_ref[:, h * dv:(h + 1) * dv].astype(F32)
        o_ref[:, h * dv:(h + 1) * dv] = (o * _silu(gate)).astype(o_ref.dtype)


def _gla_sub_blocks(v_ref, b_ref, qs_ref, kf_ref, vf_ref, acc_ref):
    C, H, dk, dv = GLA_CHUNK_ROWS, GLA_HEADS, GLA_DK, GLA_DV
    for h in range(H):
        vf_ref[h] = v_ref[:, h * dv:(h + 1) * dv].astype(F32)
    key_idx = lax.broadcasted_iota(jnp.int32, (GLA_SUB, C), 1)
    sub_row = lax.broadcasted_iota(jnp.int32, (GLA_SUB, dk), 0)

    def sub_block(i, carry):
        base = pl.multiple_of(i * GLA_SUB, GLA_SUB)
        prev = jnp.maximum(base - 1, 0)
        for h in range(H):
            b_start = b_ref[h, pl.ds(prev, 1), :]
            b_i = b_ref[h, pl.ds(base, GLA_SUB), :]
            qs_i = qs_ref[h, pl.ds(base, GLA_SUB), :]
            kx = kf_ref[h] * jnp.exp(jnp.minimum(b_start - b_ref[h], 0.0))
            qx = qs_i * jnp.exp(jnp.minimum(b_i - b_start, 0.0))
            s = _dot_nt(qx.astype(BF16), kx.astype(BF16))
            s = jnp.where(key_idx < base, s, 0.0)
            o_past = jnp.dot(s.astype(BF16), v_ref[:, h * dv:(h + 1) * dv], preferred_element_type=F32)
            terms = []
            for j in range(GLA_SUB):
                b_j = b_ref[h, pl.ds(base + j, 1), :]
                k_j = kf_ref[h, pl.ds(base + j, 1), :]
                v_j = vf_ref[h, pl.ds(base + j, 1), :]
                t = jnp.exp(jnp.minimum(b_i - b_j, 0.0)) * (qs_i * k_j)
                t = jnp.where(sub_row >= j, t, 0.0)
                terms.append(jnp.sum(t, axis=-1, keepdims=True) * v_j)
            while len(terms) > 1:
                terms = [a + b for a, b in zip(terms[0::2], terms[1::2])]
            acc_ref[h, pl.ds(base, GLA_SUB), :] += terms[0] + o_past
        return carry

    lax.fori_loop(0, C // GLA_SUB, sub_block, 0)


def _gla(p, tail, wup, bias, ng, batch, seq, layer):
    C, H, dk, dv = GLA_CHUNK_ROWS, GLA_HEADS, GLA_DK, GLA_DV
    nc = seq // C
    rows = lambda b, c: b * nc + c
    q0, k0 = PROJ_OFF[SEG_AQ] // (H * dk), PROJ_OFF[SEG_AK] // (H * dk)
    v0, g0 = PROJ_OFF[SEG_AV] // (H * dv), PROJ_OFF[SEG_AGATE] // (H * dv)
    lr0 = TAIL_OFF[SEG_ALR] // LANES
    return pl.pallas_call(
        _gla_kernel,
        grid=(batch, nc),
        in_specs=[
            pl.BlockSpec((C, H * dk), lambda b, c: (rows(b, c), q0)),
            pl.BlockSpec((C, H * dk), lambda b, c: (rows(b, c), k0)),
            pl.BlockSpec((C, H * dv), lambda b, c: (rows(b, c), v0)),
            pl.BlockSpec((C, H * dv), lambda b, c: (rows(b, c), g0)),
            pl.BlockSpec((C, LANES), lambda b, c: (rows(b, c), lr0)),
            pl.BlockSpec((None, LANES, H * dk), lambda b, c: (layer, 0, 0)),
            pl.BlockSpec((None, 1, H * dk), lambda b, c: (layer, 0, 0)),
            pl.BlockSpec((None, 1, dv), lambda b, c: (layer, 0, 0)),
        ],
        out_specs=pl.BlockSpec((C, H * dv), lambda b, c: (rows(b, c), 0)),
        out_shape=jax.ShapeDtypeStruct((batch * seq, H * dv), BF16),
        scratch_shapes=[
            pltpu.VMEM((H, dk, dv), F32),
            pltpu.VMEM((H, C, dk), F32),
            pltpu.VMEM((H, C, dk), F32),
            pltpu.VMEM((H, C, dk), BF16),
            pltpu.VMEM((H, C, dk), F32),
            pltpu.VMEM((H, C, dv), F32),
            pltpu.VMEM((H, C, dv), F32),
            pltpu.VMEM((H, C, C), BF16),
        ],
        compiler_params=pltpu.CompilerParams(dimension_semantics=("parallel", "arbitrary")),
        name="gla",
    )(p, p, p, p, tail, wup, bias, ng)


def _swa_kernel(layer, sink_ref, q_ref, kp_ref, kc_ref, vp_ref, vc_ref, gate_ref, o_ref, s_ref, p_ref):
    W, hd = SWA_WINDOW, SWA_HEAD_DIM
    group = SWA_Q_HEADS // SWA_KV_HEADS
    pairs = group // 2
    NQ = group * W
    n = pl.program_id(1)
    kk = jnp.concatenate([kp_ref[...], kc_ref[...]], axis=0)
    vv = jnp.concatenate([vp_ref[...], vc_ref[...]], axis=0)
    vt_all = jnp.transpose(vv.astype(F32))
    key_lane = lax.broadcasted_iota(jnp.int32, (2 * W, LANES), 1)
    q_lane = lax.broadcasted_iota(jnp.int32, (W, LANES), 1)
    key = lax.broadcasted_iota(jnp.int32, (2 * W, W), 0)
    qry = lax.broadcasted_iota(jnp.int32, (2 * W, W), 1)
    allowed = (key > qry) & (key <= qry + W) & ((key >= W) | (n > 0))
    head_of_col = lax.broadcasted_iota(jnp.int32, (1, NQ), 1) // W
    swap = lambda t: jnp.concatenate([t[:, hd:], t[:, :hd]], axis=1)
    low_half = lambda t, lane, first: (jnp.where(lane < hd, t, jnp.zeros_like(t)) if first
                                       else swap(jnp.where(lane >= hd, t, jnp.zeros_like(t))))
    c = (hd ** -0.5) * LOG2_E
    inv_scale = float(hd) ** 0.5
    for hk in range(SWA_KV_HEADS):
        k_lo = low_half(kk, key_lane, hk == 0)
        vt = vt_all[hk * hd:(hk + 1) * hd, :].astype(BF16)
        q_stack = jnp.concatenate(
            [low_half(q_ref[:, (hk * group + 2 * pr) * hd:(hk * group + 2 * pr + 2) * hd], q_lane, first)
             for pr in range(pairs) for first in (True, False)], axis=0)
        sink = jnp.zeros((1, NQ), F32)
        for g in range(group):
            sink = jnp.where(head_of_col == g, sink_ref[layer, hk * group + g] * inv_scale, sink)
        m = sink
        for r in range(2):
            rows = slice(r * W, (r + 1) * W)
            sc = _dot_nt(k_lo[rows], q_stack)
            for g in range(group):
                cols = slice(g * W, (g + 1) * W)
                sg = jnp.where(allowed[rows], sc[:, cols], -1e30)
                s_ref[hk, rows, cols] = sg
            m = jnp.maximum(m, jnp.max(s_ref[hk, rows, :], axis=0, keepdims=True))
        den = jnp.exp2((sink - m) * c)
        for r in range(2):
            rows = slice(r * W, (r + 1) * W)
            p = jnp.exp2((s_ref[hk, rows, :] - m) * c)
            den = den + jnp.sum(p, axis=0, keepdims=True)
            p_ref[hk, rows, :] = p.astype(BF16)
        o_t = jnp.dot(vt, p_ref[hk], preferred_element_type=F32) * (1.0 / den)
        for pr in range(pairs):
            col0 = (hk * group + 2 * pr) * hd
            o = jnp.transpose(jnp.concatenate([o_t[:, 2 * pr * W:(2 * pr + 1) * W],
                                               o_t[:, (2 * pr + 1) * W:(2 * pr + 2) * W]], axis=0))
            gate = gate_ref[:, col0:col0 + LANES].astype(F32)
            o_ref[:, col0:col0 + LANES] = (o * _silu(gate)).astype(o_ref.dtype)


def _swa(p, tail, sinks, batch, seq, layer):
    W = SWA_WINDOW
    nb = seq // W
    width = SWA_Q_HEADS * SWA_HEAD_DIM
    q0, g0 = PROJ_OFF[SEG_BQ] // width, PROJ_OFF[SEG_BGATE] // width
    k0, v0 = TAIL_OFF[SEG_BK] // LANES, TAIL_OFF[SEG_BV] // LANES
    cur = lambda b, n: b * nb + n
    prev = lambda b, n: b * nb + jnp.maximum(n - 1, 0)
    return pl.pallas_call(
        functools.partial(_swa_kernel, layer),
        grid=(batch, nb),
        in_specs=[
            pl.BlockSpec(memory_space=pltpu.SMEM),
            pl.BlockSpec((W, width), lambda b, n: (cur(b, n), q0)),
            pl.BlockSpec((W, LANES), lambda b, n: (prev(b, n), k0)),
            pl.BlockSpec((W, LANES), lambda b, n: (cur(b, n), k0)),
            pl.BlockSpec((W, LANES), lambda b, n: (prev(b, n), v0)),
            pl.BlockSpec((W, LANES), lambda b, n: (cur(b, n), v0)),
            pl.BlockSpec((W, width), lambda b, n: (cur(b, n), g0)),
        ],
        out_specs=pl.BlockSpec((W, width), lambda b, n: (cur(b, n), 0)),
        out_shape=jax.ShapeDtypeStruct((batch * seq, width), BF16),
        scratch_shapes=[
            pltpu.VMEM((SWA_KV_HEADS, 2 * W, SWA_Q_HEADS // SWA_KV_HEADS * W), F32),
            pltpu.VMEM((SWA_KV_HEADS, 2 * W, SWA_Q_HEADS // SWA_KV_HEADS * W), BF16),
        ],
        compiler_params=pltpu.CompilerParams(dimension_semantics=("parallel", "parallel")),
        name="swa",
    )(sinks, p, tail, tail, tail, tail, p)


MOBA_PENALTY = 1e30
LOG2_E = 1.4426950408889634


MOBA_HEAD_GROUP = 4


MOBA_KEY_CHUNK = 128
MOBA_SUM_ROWS = 16


def _moba_kernel(q_ref, k_ref, v_ref, gate_ref, o_ref,
                 kmean_ref, vt_ref, pen_ref, acc_ref, s_ref, p_ref, so_ref, po_ref):
    BLK, hd, HG, RC = MOBA_BLOCK, MOBA_HEAD_DIM, MOBA_HEAD_GROUP, MOBA_KEY_CHUNK
    PAIR = 2 * BLK
    n_chunks = PAIR // RC
    i = pl.program_id(2)
    nblk = k_ref.shape[0] // BLK
    heads = [slice(h * hd, (h + 1) * hd) for h in range(HG)]
    c = (hd ** -0.5) * LOG2_E

    @pl.when(i == 0)
    def _():
        for h, cols in enumerate(heads):
            kf = k_ref[:, cols].astype(F32).reshape(nblk, BLK, hd)
            rest = jnp.sum(kf, axis=1) * (1.0 / BLK)
            rest = jnp.concatenate([rest, jnp.zeros((LANES - nblk, hd), F32)], axis=0)
            for t in range(3):
                term = rest.astype(BF16)
                kmean_ref[h, t * nblk:(t + 1) * nblk, :] = term[:nblk]
                rest = rest - term.astype(F32)
            extra = lax.broadcasted_iota(jnp.int32, (MOBA_SUM_ROWS, BLK), 0)
            ones_rows = jnp.where(extra == 0, 1.0, 0.0).astype(BF16)
            for n in range(nblk):
                vt = jnp.transpose(v_ref[n * BLK:(n + 1) * BLK, cols].astype(F32)).astype(BF16)
                vt_ref[h, n // 2, :, (n % 2) * BLK:(n % 2 + 1) * BLK] = jnp.concatenate([vt, ones_rows], axis=0)

    def softmax_steps(pair, carry, next_pair=None):
        m_news, alphas = [], []
        for h in range(HG):
            m_prev, m_a, m_b = carry[h]
            pen_a = pen_ref[h, pl.ds(2 * pair, 1), :]
            pen_b = pen_ref[h, pl.ds(2 * pair + 1, 1), :]
            m_new = jnp.maximum(m_prev, jnp.maximum(m_a - pen_a, m_b - pen_b))
            for r in range(n_chunks):
                off = m_new + (pen_a if r < n_chunks // 2 else pen_b)
                p = jnp.exp2((s_ref[h, r * RC:(r + 1) * RC, :] - off) * c)
                p_ref[h, r * RC:(r + 1) * RC, :] = p.astype(BF16)
            m_news.append(m_new)
            alphas.append(jnp.exp2((m_prev - m_new) * c))
        maxes = [()] * HG
        if next_pair is not None:
            start = pl.multiple_of(next_pair * PAIR, PAIR)
            for h in range(HG):
                sc = _dot_nt(k_ref[pl.ds(start, PAIR), heads[h]], q_ref[:, heads[h]])
                s_ref[h] = sc
                maxes[h] = (jnp.max(sc[:BLK], axis=0, keepdims=True), jnp.max(sc[BLK:], axis=0, keepdims=True))
        for h in range(HG):
            acc_ref[h] = alphas[h] * acc_ref[h] + jnp.dot(vt_ref[h, pair], p_ref[h],
                                                          preferred_element_type=F32)
        return tuple((m_news[h],) + maxes[h] for h in range(HG))

    own = pl.multiple_of(i * BLK, BLK)
    key_c = lax.broadcasted_iota(jnp.int32, (RC, BLK), 0)
    qry_c = lax.broadcasted_iota(jnp.int32, (RC, BLK), 1)
    blk = lax.broadcasted_iota(jnp.int32, (nblk, BLK), 0)
    neg_inf = jnp.float32(-jnp.inf)

    gates, own_maxes, pair0_maxes = [], [], []
    for h, cols in enumerate(heads):
        q = q_ref[:, cols]
        lhs = jnp.concatenate([k_ref[pl.ds(own, BLK), cols], kmean_ref[h], k_ref[0:PAIR, cols]], axis=0)
        both = _dot_nt(lhs, q)
        gate = both[BLK:BLK + nblk] + both[BLK + nblk:BLK + 2 * nblk] + both[BLK + 2 * nblk:BLK + 3 * nblk]
        pair0 = both[BLK + 3 * nblk:]
        s_ref[h] = pair0
        pair0_max = (jnp.max(pair0[:BLK], axis=0, keepdims=True), jnp.max(pair0[BLK:], axis=0, keepdims=True))
        own_max = []
        for r in range(BLK // RC):
            rows = slice(r * RC, (r + 1) * RC)
            sc = jnp.where(key_c + r * RC <= qry_c, both[rows], -MOBA_PENALTY)
            so_ref[h, rows, :] = sc
            own_max.append(jnp.max(sc, axis=0, keepdims=True))
        gates.append(gate)
        own_maxes.append(functools.reduce(jnp.maximum, own_max))
        pair0_maxes.append(pair0_max)

    carry = []
    for h, cols in enumerate(heads):
        g = jnp.where(blk < i, gates[h], neg_inf)
        sel = jnp.zeros(g.shape, jnp.bool_)
        for _ in range(MOBA_TOPK):
            m = jnp.max(g, axis=0, keepdims=True)
            idx = jnp.min(jnp.where(g == m, blk, nblk), axis=0, keepdims=True)
            pick = (blk == idx) & (m > neg_inf)
            sel = sel | pick
            g = jnp.where(pick, neg_inf, g)
        pen_ref[h] = jnp.where(sel, 0.0, MOBA_PENALTY)

        m0 = own_maxes[h]
        for r in range(BLK // RC):
            rows = slice(r * RC, (r + 1) * RC)
            po_ref[h, rows, :] = jnp.exp2((so_ref[h, rows, :] - m0) * c).astype(BF16)
        own_vt = vt_ref[h, i // 2, :, pl.ds(pl.multiple_of((i % 2) * BLK, BLK), BLK)]
        acc_ref[h] = jnp.dot(own_vt, po_ref[h], preferred_element_type=F32)
        carry.append((m0,) + pair0_maxes[h])

    n_pairs = (i + 1) // 2

    carry = lax.fori_loop(0, jnp.maximum(n_pairs - 1, 0),
                          lambda j, carry: softmax_steps(j, carry, next_pair=j + 1), tuple(carry))
    softmax_steps(jnp.maximum(n_pairs - 1, 0), carry)
    for h, cols in enumerate(heads):
        o = jnp.transpose(acc_ref[h, 0:hd, :] * (1.0 / acc_ref[h, hd:hd + 1, :]))
        o_ref[:, cols] = (o * _silu(gate_ref[:, cols].astype(F32))).astype(o_ref.dtype)


def _moba(p, batch, seq):
    BLK, hd, HG = MOBA_BLOCK, MOBA_HEAD_DIM, MOBA_HEAD_GROUP
    nblk = seq // BLK
    assert seq % (2 * BLK) == 0 and MOBA_HEADS % HG == 0
    gw = HG * hd
    q0, k0 = PROJ_OFF[SEG_CQ] // gw, PROJ_OFF[SEG_CK] // gw
    v0, g0 = PROJ_OFF[SEG_CV] // gw, PROJ_OFF[SEG_CGATE] // gw
    scratch_bytes = HG * (seq * hd * 2 + hd * BLK * 4 + 2 * BLK * BLK * 4 + 2 * BLK * BLK * 2)
    vmem = 2 * (2 * seq * gw * 2 + 3 * BLK * gw * 2) + scratch_bytes + (12 << 20)
    return pl.pallas_call(
        _moba_kernel,
        grid=(batch, MOBA_HEADS // HG, nblk),
        in_specs=[
            pl.BlockSpec((BLK, gw), lambda b, h, i: (b * nblk + i, q0 + h)),
            pl.BlockSpec((seq, gw), lambda b, h, i: (b, k0 + h)),
            pl.BlockSpec((seq, gw), lambda b, h, i: (b, v0 + h)),
            pl.BlockSpec((BLK, gw), lambda b, h, i: (b * nblk + i, g0 + h)),
        ],
        out_specs=pl.BlockSpec((BLK, gw), lambda b, h, i: (b * nblk + i, h)),
        out_shape=jax.ShapeDtypeStruct((batch * seq, MOBA_HEADS * hd), BF16),
        scratch_shapes=[
            pltpu.VMEM((HG, 3 * nblk, hd), BF16),
            pltpu.VMEM((HG, nblk // 2, hd + MOBA_SUM_ROWS, 2 * BLK), BF16),
            pltpu.VMEM((HG, nblk, BLK), F32),
            pltpu.VMEM((HG, hd + MOBA_SUM_ROWS, BLK), F32),
            pltpu.VMEM((HG, 2 * BLK, BLK), F32),
            pltpu.VMEM((HG, 2 * BLK, BLK), BF16),
            pltpu.VMEM((HG, BLK, BLK), F32),
            pltpu.VMEM((HG, BLK, BLK), BF16),
        ],
        compiler_params=pltpu.CompilerParams(
            dimension_semantics=("parallel", "parallel", "arbitrary"), vmem_limit_bytes=vmem),
        name="moba",
    )(p, p, p, p)


MERGE_TILE_M = 256


def _merge_kernel(alpha, ya_ref, yb_ref, yc_ref, mg_ref, x_ref, wb_ref, bm_ref, wo_ref, lg_ref, lb_ref,
                  y_ref, yb16_ref):
    D = D_MODEL
    merged = None
    for n, y_n in enumerate((ya_ref, yb_ref, yc_ref)):
        up = jnp.dot(y_n[...], wb_ref[n], preferred_element_type=F32)
        gate = _sigmoid(mg_ref[:, n * D:(n + 1) * D].astype(F32) + bm_ref[n:n + 1, :])
        merged = gate * up if merged is None else merged + gate * up
    out = jnp.dot(merged.astype(BF16), wo_ref[...], preferred_element_type=F32)
    h = alpha * x_ref[...] + out
    mu = jnp.mean(h, axis=-1, keepdims=True)
    hc = h - mu
    var = jnp.mean(hc * hc, axis=-1, keepdims=True)
    y = hc * lax.rsqrt(var + LN_EPS) * lg_ref[...] + lb_ref[...]
    y_ref[...] = y
    yb16_ref[...] = y.astype(BF16)


def _merge(ya, yb, yc, p, x, wb, bm, wo, lg, lb, alpha, layer):
    m = x.shape[0]
    D, Wd = D_MODEL, BRANCH_WIDTH
    tm = min(MERGE_TILE_M, m)
    once = pl.Buffered(1)
    resident = (N_BRANCH * Wd * D + D * D) * 2
    streamed = 2 * (3 * tm * Wd * 2 + tm * 3 * D * 2 + tm * D * 4 + tm * D * 4 + tm * D * 2)
    vmem = resident + streamed + (12 << 20)
    row = lambda i: (i, 0)
    return pl.pallas_call(
        functools.partial(_merge_kernel, alpha),
        grid=(m // tm,),
        in_specs=[
            pl.BlockSpec((tm, Wd), row), pl.BlockSpec((tm, Wd), row), pl.BlockSpec((tm, Wd), row),
            pl.BlockSpec((tm, N_BRANCH * D), lambda i: (i, PROJ_OFF[SEG_MGATE] // (N_BRANCH * D))),
            pl.BlockSpec((tm, D), row),
            pl.BlockSpec((None, N_BRANCH, Wd, D), lambda i: (layer, 0, 0, 0), pipeline_mode=once),
            pl.BlockSpec((None, N_BRANCH, D), lambda i: (layer, 0, 0)),
            pl.BlockSpec((None, D, D), lambda i: (layer, 0, 0), pipeline_mode=once),
            pl.BlockSpec((None, 1, D), lambda i: (layer, 0, 0)),
            pl.BlockSpec((None, 1, D), lambda i: (layer, 0, 0)),
        ],
        out_specs=[pl.BlockSpec((tm, D), row), pl.BlockSpec((tm, D), row)],
        out_shape=[jax.ShapeDtypeStruct((m, D), F32), jax.ShapeDtypeStruct((m, D), BF16)],
        compiler_params=pltpu.CompilerParams(
            dimension_semantics=("parallel",), vmem_limit_bytes=min(vmem, V7X_VMEM_BYTES - (4 << 20))),
        name="merge_out_ln",
    )(ya, yb, yc, p, x, wb, bm, wo, lg, lb)


def kernel(x, w_in, gla_w_up, gla_b, gla_norm_g, swa_sinks, b_merge, w_branch, w_o, ln_g, ln_b):
    batch, seq, d = x.shape
    depth = w_in.shape[0]
    alpha = (2 * depth) ** 0.25
    assert d == D_MODEL and seq % (2 * MOBA_BLOCK) == 0 and seq % GLA_CHUNK_ROWS == 0

    w_in_t = jnp.swapaxes(w_in, 1, 2)
    wup = jnp.pad(gla_w_up, ((0, 0), (0, LANES - GLA_RANK), (0, 0))).astype(BF16)
    wb = w_branch.astype(BF16)
    wo = w_o.astype(BF16)
    gla_bias, gla_ng = gla_b[:, None, :], gla_norm_g[:, None, :]
    lg, lb = ln_g[:, None, :], ln_b[:, None, :]

    xf = x.reshape(batch * seq, d)
    xb = xf.astype(BF16)
    for l in range(depth):
        p = _proj(xb, w_in_t, l, PROJ_TILE_SRC, PROJ_TILE_N, "in_proj")
        tail = _proj(xb, w_in_t, l, TAIL_TILE_SRC, TAIL_TILE_N, "in_proj_tail")
        ya = _gla(p, tail, wup, gla_bias, gla_ng, batch, seq, l)
        yb = _swa(p, tail, swa_sinks, batch, seq, l)
        yc = _moba(p, batch, seq)
        xf, xb = _merge(ya, yb, yc, p, xf, wb, b_merge, wo, lg, lb, alpha, l)
    return xf.reshape(batch, seq, d)
```

```python
import functools

import jax
import jax.numpy as jnp
import numpy as np
from jax import lax
from jax.experimental import pallas as pl
from jax.experimental.pallas import tpu as pltpu

F32 = jnp.float32
BF16 = jnp.bfloat16

D_MODEL = 2048
BRANCH_WIDTH = 1024
N_BRANCH = 3
GLA_HEADS, GLA_DK, GLA_DV, GLA_RANK, GLA_TAU = 4, 128, 256, 16, 16.0
SWA_Q_HEADS, SWA_KV_HEADS, SWA_HEAD_DIM, SWA_WINDOW = 16, 2, 64, 128
MOBA_HEADS, MOBA_HEAD_DIM, MOBA_BLOCK, MOBA_TOPK = 8, 128, 256, 3
LN_EPS = 1e-5
RMS_EPS = 1e-6

IN_SPLITS = (
    GLA_HEADS * GLA_DK, GLA_HEADS * GLA_DK, GLA_HEADS * GLA_DV, GLA_RANK, BRANCH_WIDTH,
    SWA_Q_HEADS * SWA_HEAD_DIM, SWA_KV_HEADS * SWA_HEAD_DIM, SWA_KV_HEADS * SWA_HEAD_DIM, BRANCH_WIDTH,
    MOBA_HEADS * MOBA_HEAD_DIM, MOBA_HEADS * MOBA_HEAD_DIM, MOBA_HEADS * MOBA_HEAD_DIM, BRANCH_WIDTH,
    N_BRANCH * D_MODEL,
)
(SEG_AQ, SEG_AK, SEG_AV, SEG_ALR, SEG_AGATE, SEG_BQ, SEG_BK, SEG_BV, SEG_BGATE,
 SEG_CQ, SEG_CK, SEG_CV, SEG_CGATE, SEG_MGATE) = range(14)

LANES = 128
V7X_VMEM_BYTES = 64 * 1024 * 1024

PROJ_ORDER = (SEG_MGATE, SEG_AV, SEG_AGATE, SEG_BQ, SEG_BGATE, SEG_CQ, SEG_CK, SEG_CV, SEG_CGATE,
              SEG_AQ, SEG_AK)
PROJ_TILE_N = 1024
TAIL_TILE_N = 256
SRC_OFF = [int(v) for v in np.concatenate([[0], np.cumsum(IN_SPLITS)])]


def _proj_layout():
    offs, runs, cur = {}, [], 0
    for seg in PROJ_ORDER:
        offs[seg] = cur
        cur += IN_SPLITS[seg]
        if runs and runs[-1][0] + runs[-1][1] == SRC_OFF[seg]:
            runs[-1][1] += IN_SPLITS[seg]
        else:
            runs.append([SRC_OFF[seg], IN_SPLITS[seg]])
    tile_src = []
    for start, length in runs:
        assert length % PROJ_TILE_N == 0
        tile_src += [start + t for t in range(0, length, PROJ_TILE_N)]
    return offs, tile_src


PROJ_OFF, PROJ_TILE_SRC = _proj_layout()
assert SRC_OFF[SEG_BV] == SRC_OFF[SEG_BK] + IN_SPLITS[SEG_BK] and IN_SPLITS[SEG_BK] + IN_SPLITS[SEG_BV] == TAIL_TILE_N
TAIL_TILE_SRC = [SRC_OFF[SEG_BK], SRC_OFF[SEG_ALR]]
TAIL_OFF = {SEG_BK: 0, SEG_BV: IN_SPLITS[SEG_BK], SEG_ALR: TAIL_TILE_N}
PROJ_SRC_ALIGN = 16
assert all(v % PROJ_SRC_ALIGN == 0 for v in PROJ_TILE_SRC + TAIL_TILE_SRC)


def _silu(x):
    return x * (1.0 / (1.0 + jnp.exp(-x)))


def _sigmoid(x):
    return 1.0 / (1.0 + jnp.exp(-x))


def _dot_nt(a, b):
    return lax.dot_general(a, b, (((1,), (1,)), ((), ())), preferred_element_type=F32)


PROJ_TILE_M = 2048
TAIL_TILE_M = 2048


def _proj_kernel(src_ref, x_ref, wt_ref, o_ref):
    del src_ref
    o_ref[...] = _dot_nt(x_ref[...], wt_ref[0].astype(BF16)).astype(o_ref.dtype)


def _proj(xb, wt, layer, tile_src, tn, tile_m, name):
    m, d = xb.shape
    tm = min(tile_m, m)
    vmem = 2 * (tm * d * 2 + d * tn * 4 + tm * tn * 2) + d * tn * 2 + (8 << 20)
    grid_spec = pltpu.PrefetchScalarGridSpec(
        num_scalar_prefetch=1,
        grid=(m // tm, len(tile_src)),
        in_specs=[pl.BlockSpec((tm, d), lambda i, j, src: (i, 0)),
                  pl.BlockSpec((pl.Element(1), pl.Element(tn), pl.Element(d)),
                               lambda i, j, src: (layer, pl.multiple_of(src[j], PROJ_SRC_ALIGN), 0))],
        out_specs=pl.BlockSpec((tm, tn), lambda i, j, src: (i, j)),
    )
    return pl.pallas_call(
        _proj_kernel,
        grid_spec=grid_spec,
        out_shape=jax.ShapeDtypeStruct((m, len(tile_src) * tn), BF16),
        compiler_params=pltpu.CompilerParams(
            dimension_semantics=("parallel", "arbitrary"), vmem_limit_bytes=vmem),
        name=name,
    )(jnp.asarray(tile_src, jnp.int32), xb, wt)


GLA_CHUNK_ROWS = 256
GLA_SUB = 16
GLA_FAST_MAX_DECAY = 30.0
GLA_FAST_MIN_Q = 1e-20
GLA_FAST_MAX_QK = 1e20


def _gla_kernel(q_ref, k_ref, v_ref, gate_ref, alr_ref, wup_ref, bias_ref, ng_ref, o_ref,
                s_ref, b_ref, qs_ref, qx_ref, kf_ref, vf_ref, acc_ref, a_ref):
    C, H, dk, dv = GLA_CHUNK_ROWS, GLA_HEADS, GLA_DK, GLA_DV

    @pl.when(pl.program_id(1) == 0)
    def _():
        s_ref[...] = jnp.zeros_like(s_ref)

    rank_lane = lax.broadcasted_iota(jnp.int32, alr_ref.shape, 1) < GLA_RANK
    alr = jnp.where(rank_lane, alr_ref[...], jnp.zeros_like(alr_ref))
    z = jnp.dot(alr, wup_ref[...], preferred_element_type=F32) + bias_ref[...]
    g = -(jnp.maximum(-z, 0.0) + jnp.log1p(jnp.exp(-jnp.abs(z)))) * (1.0 / GLA_TAU)

    row = lax.broadcasted_iota(jnp.int32, (C, C), 0)
    col = lax.broadcasted_iota(jnp.int32, (C, C), 1)
    tril = jnp.where(col <= row, 1.0, 0.0).astype(BF16)
    g1 = g.astype(BF16)
    r1 = g - g1.astype(F32)
    g2 = r1.astype(BF16)
    g3 = (r1 - g2.astype(F32)).astype(BF16)
    b_all = (jnp.dot(tril, g1, preferred_element_type=F32)
             + jnp.dot(tril, g2, preferred_element_type=F32)
             + jnp.dot(tril, g3, preferred_element_type=F32))

    q_max = jnp.float32(0.0)
    k_max = jnp.float32(0.0)
    for h in range(H):
        b = b_all[:, h * dk:(h + 1) * dk]
        qs = q_ref[:, h * dk:(h + 1) * dk].astype(F32) * (dk ** -0.5)
        kf = k_ref[:, h * dk:(h + 1) * dk].astype(F32)
        qx = (qs * jnp.exp(b)).astype(BF16)
        b_ref[h] = b
        qs_ref[h] = qs
        kf_ref[h] = kf
        qx_ref[h] = qx
        q_max = jnp.maximum(q_max, jnp.max(jnp.abs(qs)))
        k_max = jnp.maximum(k_max, jnp.max(jnp.abs(kf)))
        acc_ref[h] = jnp.dot(qx, s_ref[h].astype(BF16), preferred_element_type=F32)

    total_decay = jnp.max(-b_all[C - 1:C, :])
    in_range = ((total_decay <= GLA_FAST_MAX_DECAY) & (q_max >= GLA_FAST_MIN_Q)
                & (q_max <= GLA_FAST_MAX_QK) & (k_max <= GLA_FAST_MAX_QK))

    @pl.when(in_range)
    def _():
        for h in range(H):
            k_inv = (kf_ref[h] * jnp.exp(-b_ref[h])).astype(BF16)
            a_ref[h] = jnp.where(col <= row, _dot_nt(qx_ref[h], k_inv), 0.0).astype(BF16)
        for h in range(H):
            acc_ref[h] += jnp.dot(a_ref[h], v_ref[:, h * dv:(h + 1) * dv], preferred_element_type=F32)

    @pl.when(jnp.logical_not(in_range))
    def _():
        _gla_sub_blocks(v_ref, b_ref, qs_ref, kf_ref, vf_ref, acc_ref)

    for h in range(H):
        b = b_ref[h]
        b_end = b[C - 1:C, :]
        k_end = kf_ref[h] * jnp.exp(b_end - b)
        decay_col = jnp.transpose(jnp.broadcast_to(jnp.exp(b_end), (dk, dk)))
        decay = jnp.concatenate([decay_col] * (dv // dk), axis=1)
        s_ref[h] = s_ref[h] * decay + jnp.dot(jnp.transpose(k_end).astype(BF16),
                                              v_ref[:, h * dv:(h + 1) * dv],
                                              preferred_element_type=F32)
        o = acc_ref[h]
        o = o * lax.rsqrt(jnp.mean(o * o, axis=-1, keepdims=True) + RMS_EPS) * ng_ref[...]
        gate = gate_ref[:, h * dv:(h + 1) * dv].astype(F32)
        o_ref[:, h * dv:(h + 1) * dv] = (o * _silu(gate)).astype(o_ref.dtype)


def _gla_sub_blocks(v_ref, b_ref, qs_ref, kf_ref, vf_ref, acc_ref):
    C, H, dk, dv = GLA_CHUNK_ROWS, GLA_HEADS, GLA_DK, GLA_DV
    for h in range(H):
        vf_ref[h] = v_ref[:, h * dv:(h + 1) * dv].astype(F32)
    key_idx = lax.broadcasted_iota(jnp.int32, (GLA_SUB, C), 1)
    sub_row = lax.broadcasted_iota(jnp.int32, (GLA_SUB, dk), 0)

    def sub_block(i, carry):
        base = pl.multiple_of(i * GLA_SUB, GLA_SUB)
        prev = jnp.maximum(base - 1, 0)
        for h in range(H):
            b_start = b_ref[h, pl.ds(prev, 1), :]
            b_i = b_ref[h, pl.ds(base, GLA_SUB), :]
            qs_i = qs_ref[h, pl.ds(base, GLA_SUB), :]
            kx = kf_ref[h] * jnp.exp(jnp.minimum(b_start - b_ref[h], 0.0))
            qx = qs_i * jnp.exp(jnp.minimum(b_i - b_start, 0.0))
            s = _dot_nt(qx.astype(BF16), kx.astype(BF16))
            s = jnp.where(key_idx < base, s, 0.0)
            o_past = jnp.dot(s.astype(BF16), v_ref[:, h * dv:(h + 1) * dv], preferred_element_type=F32)
            terms = []
            for j in range(GLA_SUB):
                b_j = b_ref[h, pl.ds(base + j, 1), :]
                k_j = kf_ref[h, pl.ds(base + j, 1), :]
                v_j = vf_ref[h, pl.ds(base + j, 1), :]
                t = jnp.exp(jnp.minimum(b_i - b_j, 0.0)) * (qs_i * k_j)
                t = jnp.where(sub_row >= j, t, 0.0)
                terms.append(jnp.sum(t, axis=-1, keepdims=True) * v_j)
            while len(terms) > 1:
                terms = [a + b for a, b in zip(terms[0::2], terms[1::2])]
            acc_ref[h, pl.ds(base, GLA_SUB), :] += terms[0] + o_past
        return carry

    lax.fori_loop(0, C // GLA_SUB, sub_block, 0)


def _gla(p, tail, wup, bias, ng, batch, seq, layer):
    C, H, dk, dv = GLA_CHUNK_ROWS, GLA_HEADS, GLA_DK, GLA_DV
    nc = seq // C
    rows = lambda b, c: b * nc + c
    q0, k0 = PROJ_OFF[SEG_AQ] // (H * dk), PROJ_OFF[SEG_AK] // (H * dk)
    v0, g0 = PROJ_OFF[SEG_AV] // (H * dv), PROJ_OFF[SEG_AGATE] // (H * dv)
    lr0 = TAIL_OFF[SEG_ALR] // LANES
    return pl.pallas_call(
        _gla_kernel,
        grid=(batch, nc),
        in_specs=[
            pl.BlockSpec((C, H * dk), lambda b, c: (rows(b, c), q0)),
            pl.BlockSpec((C, H * dk), lambda b, c: (rows(b, c), k0)),
            pl.BlockSpec((C, H * dv), lambda b, c: (rows(b, c), v0)),
            pl.BlockSpec((C, H * dv), lambda b, c: (rows(b, c), g0)),
            pl.BlockSpec((C, LANES), lambda b, c: (rows(b, c), lr0)),
            pl.BlockSpec((None, LANES, H * dk), lambda b, c: (layer, 0, 0)),
            pl.BlockSpec((None, 1, H * dk), lambda b, c: (layer, 0, 0)),
            pl.BlockSpec((None, 1, dv), lambda b, c: (layer, 0, 0)),
        ],
        out_specs=pl.BlockSpec((C, H * dv), lambda b, c: (rows(b, c), 0)),
        out_shape=jax.ShapeDtypeStruct((batch * seq, H * dv), BF16),
        scratch_shapes=[
            pltpu.VMEM((H, dk, dv), F32),
            pltpu.VMEM((H, C, dk), F32),
            pltpu.VMEM((H, C, dk), F32),
            pltpu.VMEM((H, C, dk), BF16),
            pltpu.VMEM((H, C, dk), F32),
            pltpu.VMEM((H, C, dv), F32),
            pltpu.VMEM((H, C, dv), F32),
            pltpu.VMEM((H, C, C), BF16),
        ],
        compiler_params=pltpu.CompilerParams(dimension_semantics=("parallel", "arbitrary")),
        name="gla",
    )(p, p, p, p, tail, wup, bias, ng)


SWA_BLOCKS_PER_STEP = 2
SWA_MASKED = -1e30


def _swa_kernel(layer, sink_ref, q_ref, kp_ref, kc_ref, vp_ref, vc_ref, gate_ref, o_ref, s_ref, p_ref):
    W, hd, NB = SWA_WINDOW, SWA_HEAD_DIM, SWA_BLOCKS_PER_STEP
    group = SWA_Q_HEADS // SWA_KV_HEADS
    pairs = group // 2
    NQ = group * W
    n = pl.program_id(1)
    kk = jnp.concatenate([kp_ref[...], kc_ref[...]], axis=0)
    vv = jnp.concatenate([vp_ref[...], vc_ref[...]], axis=0)
    vt_all = jnp.transpose(vv.astype(F32))
    key_lane = lax.broadcasted_iota(jnp.int32, kk.shape, 1)
    q_lane = lax.broadcasted_iota(jnp.int32, (W, LANES), 1)
    key = lax.broadcasted_iota(jnp.int32, (2 * W, W), 0)
    qry = lax.broadcasted_iota(jnp.int32, (2 * W, W), 1)
    band = (key > qry) & (key <= qry + W)
    head_of_col = lax.broadcasted_iota(jnp.int32, (1, NQ), 1) // W
    swap = lambda t: jnp.concatenate([t[:, hd:], t[:, :hd]], axis=1)
    low_half = lambda t, lane, first: (jnp.where(lane < hd, t, jnp.zeros_like(t)) if first
                                       else swap(jnp.where(lane >= hd, t, jnp.zeros_like(t))))
    c = (hd ** -0.5) * LOG2_E
    inv_scale = float(hd) ** 0.5
    k_low = [low_half(kk, key_lane, hk == 0) for hk in range(SWA_KV_HEADS)]
    for sub in range(NB):
        allowed = band & ((key >= W) | (n * NB + sub > 0))
        q_rows = slice(sub * W, (sub + 1) * W)
        for hk in range(SWA_KV_HEADS):
            slot = sub * SWA_KV_HEADS + hk
            k_lo = k_low[hk][sub * W:(sub + 2) * W]
            vt = vt_all[hk * hd:(hk + 1) * hd, sub * W:(sub + 2) * W].astype(BF16)
            q_stack = jnp.concatenate(
                [low_half(q_ref[q_rows, (hk * group + 2 * pr) * hd:(hk * group + 2 * pr + 2) * hd], q_lane, first)
                 for pr in range(pairs) for first in (True, False)], axis=0)
            sink = jnp.zeros((1, NQ), F32)
            for g in range(group):
                sink = jnp.where(head_of_col == g, sink_ref[layer, hk * group + g] * inv_scale, sink)
            m = sink
            for r in range(2):
                rows = slice(r * W, (r + 1) * W)
                sc = _dot_nt(k_lo[rows], q_stack)
                for g in range(group):
                    cols = slice(g * W, (g + 1) * W)
                    s_ref[slot, rows, cols] = jnp.where(allowed[rows], sc[:, cols], SWA_MASKED)
                m = jnp.maximum(m, jnp.max(s_ref[slot, rows, :], axis=0, keepdims=True))
            den = jnp.exp2((sink - m) * c)
            for r in range(2):
                rows = slice(r * W, (r + 1) * W)
                p = jnp.exp2((s_ref[slot, rows, :] - m) * c)
                den = den + jnp.sum(p, axis=0, keepdims=True)
                p_ref[slot, rows, :] = p.astype(BF16)
            o_t = jnp.dot(vt, p_ref[slot], preferred_element_type=F32) * (1.0 / den)
            for pr in range(pairs):
                col0 = (hk * group + 2 * pr) * hd
                o = jnp.transpose(jnp.concatenate([o_t[:, 2 * pr * W:(2 * pr + 1) * W],
                                                   o_t[:, (2 * pr + 1) * W:(2 * pr + 2) * W]], axis=0))
                gate = gate_ref[q_rows, col0:col0 + LANES].astype(F32)
                o_ref[q_rows, col0:col0 + LANES] = (o * _silu(gate)).astype(o_ref.dtype)


def _swa(p, tail, sinks, batch, seq, layer):
    W, NB = SWA_WINDOW, SWA_BLOCKS_PER_STEP
    steps = seq // (NB * W)
    assert seq % (NB * W) == 0
    width = SWA_Q_HEADS * SWA_HEAD_DIM
    q0, g0 = PROJ_OFF[SEG_BQ] // width, PROJ_OFF[SEG_BGATE] // width
    k0, v0 = TAIL_OFF[SEG_BK] // LANES, TAIL_OFF[SEG_BV] // LANES
    cur = lambda b, n: b * steps + n
    prev = lambda b, n: (b * steps + n) * NB - jnp.minimum(n, 1)
    slots = NB * SWA_KV_HEADS
    nq = SWA_Q_HEADS // SWA_KV_HEADS * W
    return pl.pallas_call(
        functools.partial(_swa_kernel, layer),
        grid=(batch, steps),
        in_specs=[
            pl.BlockSpec(memory_space=pltpu.SMEM),
            pl.BlockSpec((NB * W, width), lambda b, n: (cur(b, n), q0)),
            pl.BlockSpec((W, LANES), lambda b, n: (prev(b, n), k0)),
            pl.BlockSpec((NB * W, LANES), lambda b, n: (cur(b, n), k0)),
            pl.BlockSpec((W, LANES), lambda b, n: (prev(b, n), v0)),
            pl.BlockSpec((NB * W, LANES), lambda b, n: (cur(b, n), v0)),
            pl.BlockSpec((NB * W, width), lambda b, n: (cur(b, n), g0)),
        ],
        out_specs=pl.BlockSpec((NB * W, width), lambda b, n: (cur(b, n), 0)),
        out_shape=jax.ShapeDtypeStruct((batch * seq, width), BF16),
        scratch_shapes=[
            pltpu.VMEM((slots, 2 * W, nq), F32),
            pltpu.VMEM((slots, 2 * W, nq), BF16),
        ],
        compiler_params=pltpu.CompilerParams(dimension_semantics=("parallel", "parallel")),
        name="swa",
    )(sinks, p, tail, tail, tail, tail, p)


MOBA_PENALTY = 1e30
LOG2_E = 1.4426950408889634


MOBA_HEAD_GROUP = 4


MOBA_KEY_CHUNK = 128
MOBA_SUM_ROWS = 16


def _moba_kernel(q_ref, k_ref, v_ref, gate_ref, o_ref,
                 kmean_ref, vt_ref, pen_ref, acc_ref, s_ref, p_ref, so_ref, po_ref):
    BLK, hd, HG, RC = MOBA_BLOCK, MOBA_HEAD_DIM, MOBA_HEAD_GROUP, MOBA_KEY_CHUNK
    PAIR = 2 * BLK
    n_chunks = PAIR // RC
    i = pl.program_id(2)
    nblk = k_ref.shape[0] // BLK
    heads = [slice(h * hd, (h + 1) * hd) for h in range(HG)]
    c = (hd ** -0.5) * LOG2_E

    @pl.when(i == 0)
    def _():
        for h, cols in enumerate(heads):
            kf = k_ref[:, cols].astype(F32).reshape(nblk, BLK, hd)
            rest = jnp.sum(kf, axis=1) * (1.0 / BLK)
            rest = jnp.concatenate([rest, jnp.zeros((LANES - nblk, hd), F32)], axis=0)
            for t in range(3):
                term = rest.astype(BF16)
                kmean_ref[h, t * nblk:(t + 1) * nblk, :] = term[:nblk]
                rest = rest - term.astype(F32)
            extra = lax.broadcasted_iota(jnp.int32, (MOBA_SUM_ROWS, BLK), 0)
            ones_rows = jnp.where(extra == 0, 1.0, 0.0).astype(BF16)
            for n in range(nblk):
                vt = jnp.transpose(v_ref[n * BLK:(n + 1) * BLK, cols].astype(F32)).astype(BF16)
                vt_ref[h, n // 2, :, (n % 2) * BLK:(n % 2 + 1) * BLK] = jnp.concatenate([vt, ones_rows], axis=0)

    def softmax_steps(pair, carry, next_pair=None):
        m_news, alphas = [], []
        for h in range(HG):
            m_prev, m_a, m_b = carry[h]
            pen_a = pen_ref[h, pl.ds(2 * pair, 1), :]
            pen_b = pen_ref[h, pl.ds(2 * pair + 1, 1), :]
            m_new = jnp.maximum(m_prev, jnp.maximum(m_a - pen_a, m_b - pen_b))
            for r in range(n_chunks):
                off = m_new + (pen_a if r < n_chunks // 2 else pen_b)
                p = jnp.exp2((s_ref[h, r * RC:(r + 1) * RC, :] - off) * c)
                p_ref[h, r * RC:(r + 1) * RC, :] = p.astype(BF16)
            m_news.append(m_new)
            alphas.append(jnp.exp2((m_prev - m_new) * c))
        maxes = [()] * HG
        if next_pair is not None:
            start = pl.multiple_of(next_pair * PAIR, PAIR)
            for h in range(HG):
                sc = _dot_nt(k_ref[pl.ds(start, PAIR), heads[h]], q_ref[:, heads[h]])
                s_ref[h] = sc
                maxes[h] = (jnp.max(sc[:BLK], axis=0, keepdims=True), jnp.max(sc[BLK:], axis=0, keepdims=True))
        for h in range(HG):
            acc_ref[h] = alphas[h] * acc_ref[h] + jnp.dot(vt_ref[h, pair], p_ref[h],
                                                          preferred_element_type=F32)
        return tuple((m_news[h],) + maxes[h] for h in range(HG))

    own = pl.multiple_of(i * BLK, BLK)
    key_c = lax.broadcasted_iota(jnp.int32, (RC, BLK), 0)
    qry_c = lax.broadcasted_iota(jnp.int32, (RC, BLK), 1)
    blk = lax.broadcasted_iota(jnp.int32, (nblk, BLK), 0)
    neg_inf = jnp.float32(-jnp.inf)

    gates, own_maxes, pair0_maxes = [], [], []
    for h, cols in enumerate(heads):
        q = q_ref[:, cols]
        lhs = jnp.concatenate([k_ref[pl.ds(own, BLK), cols], kmean_ref[h], k_ref[0:PAIR, cols]], axis=0)
        both = _dot_nt(lhs, q)
        gate = both[BLK:BLK + nblk] + both[BLK + nblk:BLK + 2 * nblk] + both[BLK + 2 * nblk:BLK + 3 * nblk]
        pair0 = both[BLK + 3 * nblk:]
        s_ref[h] = pair0
        pair0_max = (jnp.max(pair0[:BLK], axis=0, keepdims=True), jnp.max(pair0[BLK:], axis=0, keepdims=True))
        own_max = []
        for r in range(BLK // RC):
            rows = slice(r * RC, (r + 1) * RC)
            sc = jnp.where(key_c + r * RC <= qry_c, both[rows], -MOBA_PENALTY)
            so_ref[h, rows, :] = sc
            own_max.append(jnp.max(sc, axis=0, keepdims=True))
        gates.append(gate)
        own_maxes.append(functools.reduce(jnp.maximum, own_max))
        pair0_maxes.append(pair0_max)

    carry = []
    for h, cols in enumerate(heads):
        g = jnp.where(blk < i, gates[h], neg_inf)
        sel = jnp.zeros(g.shape, jnp.bool_)
        for _ in range(MOBA_TOPK):
            m = jnp.max(g, axis=0, keepdims=True)
            idx = jnp.min(jnp.where(g == m, blk, nblk), axis=0, keepdims=True)
            pick = (blk == idx) & (m > neg_inf)
            sel = sel | pick
            g = jnp.where(pick, neg_inf, g)
        pen_ref[h] = jnp.where(sel, 0.0, MOBA_PENALTY)

        m0 = own_maxes[h]
        for r in range(BLK // RC):
            rows = slice(r * RC, (r + 1) * RC)
            po_ref[h, rows, :] = jnp.exp2((so_ref[h, rows, :] - m0) * c).astype(BF16)
        own_vt = vt_ref[h, i // 2, :, pl.ds(pl.multiple_of((i % 2) * BLK, BLK), BLK)]
        acc_ref[h] = jnp.dot(own_vt, po_ref[h], preferred_element_type=F32)
        carry.append((m0,) + pair0_maxes[h])

    n_pairs = (i + 1) // 2

    carry = lax.fori_loop(0, jnp.maximum(n_pairs - 1, 0),
                          lambda j, carry: softmax_steps(j, carry, next_pair=j + 1), tuple(carry))
    softmax_steps(jnp.maximum(n_pairs - 1, 0), carry)
    for h, cols in enumerate(heads):
        o = jnp.transpose(acc_ref[h, 0:hd, :] * (1.0 / acc_ref[h, hd:hd + 1, :]))
        o_ref[:, cols] = (o * _silu(gate_ref[:, cols].astype(F32))).astype(o_ref.dtype)


def _moba(p, batch, seq):
    BLK, hd, HG = MOBA_BLOCK, MOBA_HEAD_DIM, MOBA_HEAD_GROUP
    nblk = seq // BLK
    assert seq % (2 * BLK) == 0 and MOBA_HEADS % HG == 0
    gw = HG * hd
    q0, k0 = PROJ_OFF[SEG_CQ] // gw, PROJ_OFF[SEG_CK] // gw
    v0, g0 = PROJ_OFF[SEG_CV] // gw, PROJ_OFF[SEG_CGATE] // gw
    scratch_bytes = HG * (seq * hd * 2 + hd * BLK * 4 + 2 * BLK * BLK * 4 + 2 * BLK * BLK * 2)
    vmem = 2 * (2 * seq * gw * 2 + 3 * BLK * gw * 2) + scratch_bytes + (12 << 20)
    return pl.pallas_call(
        _moba_kernel,
        grid=(batch, MOBA_HEADS // HG, nblk),
        in_specs=[
            pl.BlockSpec((BLK, gw), lambda b, h, i: (b * nblk + i, q0 + h)),
            pl.BlockSpec((seq, gw), lambda b, h, i: (b, k0 + h)),
            pl.BlockSpec((seq, gw), lambda b, h, i: (b, v0 + h)),
            pl.BlockSpec((BLK, gw), lambda b, h, i: (b * nblk + i, g0 + h)),
        ],
        out_specs=pl.BlockSpec((BLK, gw), lambda b, h, i: (b * nblk + i, h)),
        out_shape=jax.ShapeDtypeStruct((batch * seq, MOBA_HEADS * hd), BF16),
        scratch_shapes=[
            pltpu.VMEM((HG, 3 * nblk, hd), BF16),
            pltpu.VMEM((HG, nblk // 2, hd + MOBA_SUM_ROWS, 2 * BLK), BF16),
            pltpu.VMEM((HG, nblk, BLK), F32),
            pltpu.VMEM((HG, hd + MOBA_SUM_ROWS, BLK), F32),
            pltpu.VMEM((HG, 2 * BLK, BLK), F32),
            pltpu.VMEM((HG, 2 * BLK, BLK), BF16),
            pltpu.VMEM((HG, BLK, BLK), F32),
            pltpu.VMEM((HG, BLK, BLK), BF16),
        ],
        compiler_params=pltpu.CompilerParams(
            dimension_semantics=("parallel", "parallel", "arbitrary"), vmem_limit_bytes=vmem),
        name="moba",
    )(p, p, p, p)


MERGE_TILE_M = 256


def _merge_kernel(alpha, ya_ref, yb_ref, yc_ref, mg_ref, x_ref, wb_ref, bm_ref, wo_ref, lg_ref, lb_ref,
                  y_ref, yb16_ref):
    D = D_MODEL
    merged = None
    for n, y_n in enumerate((ya_ref, yb_ref, yc_ref)):
        up = jnp.dot(y_n[...], wb_ref[n], preferred_element_type=F32)
        gate = _sigmoid(mg_ref[:, n * D:(n + 1) * D].astype(F32) + bm_ref[n:n + 1, :])
        merged = gate * up if merged is None else merged + gate * up
    out = jnp.dot(merged.astype(BF16), wo_ref[...], preferred_element_type=F32)
    h = alpha * x_ref[...] + out
    mu = jnp.mean(h, axis=-1, keepdims=True)
    hc = h - mu
    var = jnp.mean(hc * hc, axis=-1, keepdims=True)
    y = hc * lax.rsqrt(var + LN_EPS) * lg_ref[...] + lb_ref[...]
    y_ref[...] = y
    yb16_ref[...] = y.astype(BF16)


def _merge(ya, yb, yc, p, x, wb, bm, wo, lg, lb, alpha, layer):
    m = x.shape[0]
    D, Wd = D_MODEL, BRANCH_WIDTH
    tm = min(MERGE_TILE_M, m)
    once = pl.Buffered(1)
    resident = (N_BRANCH * Wd * D + D * D) * 2
    streamed = 2 * (3 * tm * Wd * 2 + tm * 3 * D * 2 + tm * D * 4 + tm * D * 4 + tm * D * 2)
    vmem = resident + streamed + (12 << 20)
    row = lambda i: (i, 0)
    return pl.pallas_call(
        functools.partial(_merge_kernel, alpha),
        grid=(m // tm,),
        in_specs=[
            pl.BlockSpec((tm, Wd), row), pl.BlockSpec((tm, Wd), row), pl.BlockSpec((tm, Wd), row),
            pl.BlockSpec((tm, N_BRANCH * D), lambda i: (i, PROJ_OFF[SEG_MGATE] // (N_BRANCH * D))),
            pl.BlockSpec((tm, D), row),
            pl.BlockSpec((None, N_BRANCH, Wd, D), lambda i: (layer, 0, 0, 0), pipeline_mode=once),
            pl.BlockSpec((None, N_BRANCH, D), lambda i: (layer, 0, 0)),
            pl.BlockSpec((None, D, D), lambda i: (layer, 0, 0), pipeline_mode=once),
            pl.BlockSpec((None, 1, D), lambda i: (layer, 0, 0)),
            pl.BlockSpec((None, 1, D), lambda i: (layer, 0, 0)),
        ],
        out_specs=[pl.BlockSpec((tm, D), row), pl.BlockSpec((tm, D), row)],
        out_shape=[jax.ShapeDtypeStruct((m, D), F32), jax.ShapeDtypeStruct((m, D), BF16)],
        compiler_params=pltpu.CompilerParams(
            dimension_semantics=("parallel",), vmem_limit_bytes=min(vmem, V7X_VMEM_BYTES - (4 << 20))),
        name="merge_out_ln",
    )(ya, yb, yc, p, x, wb, bm, wo, lg, lb)


def kernel(x, w_in, gla_w_up, gla_b, gla_norm_g, swa_sinks, b_merge, w_branch, w_o, ln_g, ln_b):
    batch, seq, d = x.shape
    depth = w_in.shape[0]
    alpha = (2 * depth) ** 0.25
    assert d == D_MODEL and seq % (2 * MOBA_BLOCK) == 0 and seq % GLA_CHUNK_ROWS == 0

    w_in_t = jnp.swapaxes(w_in, 1, 2)
    wup = jnp.pad(gla_w_up, ((0, 0), (0, LANES - GLA_RANK), (0, 0))).astype(BF16)
    wb = w_branch.astype(BF16)
    wo = w_o.astype(BF16)
    gla_bias, gla_ng = gla_b[:, None, :], gla_norm_g[:, None, :]
    lg, lb = ln_g[:, None, :], ln_b[:, None, :]

    xf = x.reshape(batch * seq, d)
    xb = xf.astype(BF16)
    for l in range(depth):
        p = _proj(xb, w_in_t, l, PROJ_TILE_SRC, PROJ_TILE_N, PROJ_TILE_M, "in_proj")
        tail = _proj(xb, w_in_t, l, TAIL_TILE_SRC, TAIL_TILE_N, TAIL_TILE_M, "in_proj_tail")
        ya = _gla(p, tail, wup, gla_bias, gla_ng, batch, seq, l)
        yb = _swa(p, tail, swa_sinks, batch, seq, l)
        yc = _moba(p, batch, seq)
        xf, xb = _merge(ya, yb, yc, p, xf, wb, b_merge, wo, lg, lb, alpha, l)
    return xf.reshape(batch, seq, d)
```

```python
import functools

import jax
import jax.numpy as jnp
import numpy as np
from jax import lax
from jax.experimental import pallas as pl
from jax.experimental.pallas import tpu as pltpu

F32 = jnp.float32
BF16 = jnp.bfloat16

D_MODEL = 2048
BRANCH_WIDTH = 1024
N_BRANCH = 3
GLA_HEADS, GLA_DK, GLA_DV, GLA_RANK, GLA_TAU = 4, 128, 256, 16, 16.0
SWA_Q_HEADS, SWA_KV_HEADS, SWA_HEAD_DIM, SWA_WINDOW = 16, 2, 64, 128
MOBA_HEADS, MOBA_HEAD_DIM, MOBA_BLOCK, MOBA_TOPK = 8, 128, 256, 3
LN_EPS = 1e-5
RMS_EPS = 1e-6

IN_SPLITS = (
    GLA_HEADS * GLA_DK, GLA_HEADS * GLA_DK, GLA_HEADS * GLA_DV, GLA_RANK, BRANCH_WIDTH,
    SWA_Q_HEADS * SWA_HEAD_DIM, SWA_KV_HEADS * SWA_HEAD_DIM, SWA_KV_HEADS * SWA_HEAD_DIM, BRANCH_WIDTH,
    MOBA_HEADS * MOBA_HEAD_DIM, MOBA_HEADS * MOBA_HEAD_DIM, MOBA_HEADS * MOBA_HEAD_DIM, BRANCH_WIDTH,
    N_BRANCH * D_MODEL,
)
(SEG_AQ, SEG_AK, SEG_AV, SEG_ALR, SEG_AGATE, SEG_BQ, SEG_BK, SEG_BV, SEG_BGATE,
 SEG_CQ, SEG_CK, SEG_CV, SEG_CGATE, SEG_MGATE) = range(14)

LANES = 128
V7X_VMEM_BYTES = 64 * 1024 * 1024

PROJ_ORDER = (SEG_MGATE, SEG_AV, SEG_AGATE, SEG_BQ, SEG_BGATE, SEG_CQ, SEG_CK, SEG_CV, SEG_CGATE,
              SEG_AQ, SEG_AK)
PROJ_TILE_N = 1024
TAIL_TILE_N = 256
SRC_OFF = [int(v) for v in np.concatenate([[0], np.cumsum(IN_SPLITS)])]


def _proj_layout():
    offs, runs, cur = {}, [], 0
    for seg in PROJ_ORDER:
        offs[seg] = cur
        cur += IN_SPLITS[seg]
        if runs and runs[-1][0] + runs[-1][1] == SRC_OFF[seg]:
            runs[-1][1] += IN_SPLITS[seg]
        else:
            runs.append([SRC_OFF[seg], IN_SPLITS[seg]])
    tile_src = []
    for start, length in runs:
        assert length % PROJ_TILE_N == 0
        tile_src += [start + t for t in range(0, length, PROJ_TILE_N)]
    return offs, tile_src


PROJ_OFF, PROJ_TILE_SRC = _proj_layout()
assert SRC_OFF[SEG_BV] == SRC_OFF[SEG_BK] + IN_SPLITS[SEG_BK] and IN_SPLITS[SEG_BK] + IN_SPLITS[SEG_BV] == TAIL_TILE_N
TAIL_TILE_SRC = [SRC_OFF[SEG_BK], SRC_OFF[SEG_ALR]]
TAIL_OFF = {SEG_BK: 0, SEG_BV: IN_SPLITS[SEG_BK], SEG_ALR: TAIL_TILE_N}
PROJ_SRC_ALIGN = 16
assert all(v % PROJ_SRC_ALIGN == 0 for v in PROJ_TILE_SRC + TAIL_TILE_SRC)


def _silu(x):
    return x * (1.0 / (1.0 + jnp.exp(-x)))


def _sigmoid(x):
    return 1.0 / (1.0 + jnp.exp(-x))


def _dot_nt(a, b):
    return lax.dot_general(a, b, (((1,), (1,)), ((), ())), preferred_element_type=F32)


PROJ_TILE_M = 2048
TAIL_TILE_M = 2048


def _proj_kernel(src_ref, x_ref, wt_ref, o_ref):
    del src_ref
    o_ref[...] = _dot_nt(x_ref[...], wt_ref[0].astype(BF16)).astype(o_ref.dtype)


def _proj(xb, wt, layer, tile_src, tn, tile_m, name):
    m, d = xb.shape
    tm = min(tile_m, m)
    vmem = 2 * (tm * d * 2 + d * tn * 4 + tm * tn * 2) + d * tn * 2 + (8 << 20)
    grid_spec = pltpu.PrefetchScalarGridSpec(
        num_scalar_prefetch=1,
        grid=(m // tm, len(tile_src)),
        in_specs=[pl.BlockSpec((tm, d), lambda i, j, src: (i, 0)),
                  pl.BlockSpec((pl.Element(1), pl.Element(tn), pl.Element(d)),
                               lambda i, j, src: (layer, pl.multiple_of(src[j], PROJ_SRC_ALIGN), 0))],
        out_specs=pl.BlockSpec((tm, tn), lambda i, j, src: (i, j)),
    )
    return pl.pallas_call(
        _proj_kernel,
        grid_spec=grid_spec,
        out_shape=jax.ShapeDtypeStruct((m, len(tile_src) * tn), BF16),
        compiler_params=pltpu.CompilerParams(
            dimension_semantics=("parallel", "arbitrary"), vmem_limit_bytes=vmem),
        name=name,
    )(jnp.asarray(tile_src, jnp.int32), xb, wt)


def _proj_cast_kernel(src_ref, x_ref, wt_ref, wb_ref, wo_ref, o_ref, wb16_ref, wo16_ref):
    del src_ref
    o_ref[...] = _dot_nt(x_ref[...], wt_ref[0].astype(BF16)).astype(o_ref.dtype)
    wb16_ref[...] = wb_ref[...].astype(BF16)
    wo16_ref[...] = wo_ref[...].astype(BF16)


def _proj_and_cast(xb, wt, w_branch, w_o, layer):
    m, d = xb.shape
    tm, tn, tile_src = min(PROJ_TILE_M, m), PROJ_TILE_N, PROJ_TILE_SRC
    n_tiles = len(tile_src)
    rows_b, rows_o = w_branch.shape[1], w_o.shape[1]
    steps = (m // tm) * n_tiles
    slab = next(r for r in (16 << k for k in range(12))
                if rows_b % r == 0 and rows_o % r == 0 and steps * r >= max(rows_b, rows_o))
    slab_b = lambda i, j: jnp.minimum(i * n_tiles + j, rows_b // slab - 1)
    slab_o = lambda i, j: jnp.minimum(i * n_tiles + j, rows_o // slab - 1)
    vmem = (2 * (tm * d * 2 + d * tn * 4 + tm * tn * 2) + d * tn * 2 + 4 * 2 * slab * d * 6 + (8 << 20))
    grid_spec = pltpu.PrefetchScalarGridSpec(
        num_scalar_prefetch=1,
        grid=(m // tm, n_tiles),
        in_specs=[pl.BlockSpec((tm, d), lambda i, j, src: (i, 0)),
                  pl.BlockSpec((pl.Element(1), pl.Element(tn), pl.Element(d)),
                               lambda i, j, src: (layer, pl.multiple_of(src[j], PROJ_SRC_ALIGN), 0)),
                  pl.BlockSpec((None, slab, d), lambda i, j, src: (layer, slab_b(i, j), 0)),
                  pl.BlockSpec((None, slab, d), lambda i, j, src: (layer, slab_o(i, j), 0))],
        out_specs=[pl.BlockSpec((tm, tn), lambda i, j, src: (i, j)),
                   pl.BlockSpec((slab, d), lambda i, j, src: (slab_b(i, j), 0)),
                   pl.BlockSpec((slab, d), lambda i, j, src: (slab_o(i, j), 0))],
    )
    return pl.pallas_call(
        _proj_cast_kernel,
        grid_spec=grid_spec,
        out_shape=[jax.ShapeDtypeStruct((m, n_tiles * tn), BF16),
                   jax.ShapeDtypeStruct((rows_b, d), BF16), jax.ShapeDtypeStruct((rows_o, d), BF16)],
        compiler_params=pltpu.CompilerParams(
            dimension_semantics=("arbitrary", "arbitrary"), vmem_limit_bytes=vmem),
        name="in_proj",
    )(jnp.asarray(tile_src, jnp.int32), xb, wt, w_branch, w_o)


GLA_CHUNK_ROWS = 256
GLA_SUB = 16
GLA_FAST_MAX_DECAY = 30.0
GLA_FAST_MIN_Q = 1e-20
GLA_FAST_MAX_QK = 1e20


def _gla_kernel(q_ref, k_ref, v_ref, gate_ref, alr_ref, wup_ref, bias_ref, ng_ref, o_ref,
                s_ref, b_ref, qs_ref, qx_ref, kf_ref, vf_ref, acc_ref, a_ref):
    C, H, dk, dv = GLA_CHUNK_ROWS, GLA_HEADS, GLA_DK, GLA_DV

    @pl.when(pl.program_id(1) == 0)
    def _():
        s_ref[...] = jnp.zeros_like(s_ref)

    rank_lane = lax.broadcasted_iota(jnp.int32, alr_ref.shape, 1) < GLA_RANK
    alr = jnp.where(rank_lane, alr_ref[...], jnp.zeros_like(alr_ref))
    z = jnp.dot(alr, wup_ref[...], preferred_element_type=F32) + bias_ref[...]
    g = -(jnp.maximum(-z, 0.0) + jnp.log1p(jnp.exp(-jnp.abs(z)))) * (1.0 / GLA_TAU)

    row = lax.broadcasted_iota(jnp.int32, (C, C), 0)
    col = lax.broadcasted_iota(jnp.int32, (C, C), 1)
    tril = jnp.where(col <= row, 1.0, 0.0).astype(BF16)
    g1 = g.astype(BF16)
    r1 = g - g1.astype(F32)
    g2 = r1.astype(BF16)
    g3 = (r1 - g2.astype(F32)).astype(BF16)
    b_all = (jnp.dot(tril, g1, preferred_element_type=F32)
             + jnp.dot(tril, g2, preferred_element_type=F32)
             + jnp.dot(tril, g3, preferred_element_type=F32))

    q_max = jnp.float32(0.0)
    k_max = jnp.float32(0.0)
    for h in range(H):
        b = b_all[:, h * dk:(h + 1) * dk]
        qs = q_ref[:, h * dk:(h + 1) * dk].astype(F32) * (dk ** -0.5)
        kf = k_ref[:, h * dk:(h + 1) * dk].astype(F32)
        qx = (qs * jnp.exp(b)).astype(BF16)
        b_ref[h] = b
        qs_ref[h] = qs
        kf_ref[h] = kf
        qx_ref[h] = qx
        q_max = jnp.maximum(q_max, jnp.max(jnp.abs(qs)))
        k_max = jnp.maximum(k_max, jnp.max(jnp.abs(kf)))
        acc_ref[h] = jnp.dot(qx, s_ref[h].astype(BF16), preferred_element_type=F32)

    total_decay = jnp.max(-b_all[C - 1:C, :])
    in_range = ((total_decay <= GLA_FAST_MAX_DECAY) & (q_max >= GLA_FAST_MIN_Q)
                & (q_max <= GLA_FAST_MAX_QK) & (k_max <= GLA_FAST_MAX_QK))

    @pl.when(in_range)
    def _():
        for h in range(H):
            k_inv = (kf_ref[h] * jnp.exp(-b_ref[h])).astype(BF16)
            a_ref[h] = jnp.where(col <= row, _dot_nt(qx_ref[h], k_inv), 0.0).astype(BF16)
        for h in range(H):
            acc_ref[h] += jnp.dot(a_ref[h], v_ref[:, h * dv:(h + 1) * dv], preferred_element_type=F32)

    @pl.when(jnp.logical_not(in_range))
    def _():
        _gla_sub_blocks(v_ref, b_ref, qs_ref, kf_ref, vf_ref, acc_ref)

    for h in range(H):
        b = b_ref[h]
        b_end = b[C - 1:C, :]
        k_end = kf_ref[h] * jnp.exp(b_end - b)
        decay_col = jnp.transpose(jnp.broadcast_to(jnp.exp(b_end), (dk, dk)))
        decay = jnp.concatenate([decay_col] * (dv // dk), axis=1)
        s_ref[h] = s_ref[h] * decay + jnp.dot(jnp.transpose(k_end).astype(BF16),
                                              v_ref[:, h * dv:(h + 1) * dv],
                                              preferred_element_type=F32)
        o = acc_ref[h]
        o = o * lax.rsqrt(jnp.mean(o * o, axis=-1, keepdims=True) + RMS_EPS) * ng_ref[...]
        gate = gate_ref[:, h * dv:(h + 1) * dv].astype(F32)
        o_ref[:, h * dv:(h + 1) * dv] = (o * _silu(gate)).astype(o_ref.dtype)


def _gla_sub_blocks(v_ref, b_ref, qs_ref, kf_ref, vf_ref, acc_ref):
    C, H, dk, dv = GLA_CHUNK_ROWS, GLA_HEADS, GLA_DK, GLA_DV
    for h in range(H):
        vf_ref[h] = v_ref[:, h * dv:(h + 1) * dv].astype(F32)
    key_idx = lax.broadcasted_iota(jnp.int32, (GLA_SUB, C), 1)
    sub_row = lax.broadcasted_iota(jnp.int32, (GLA_SUB, dk), 0)

    def sub_block(i, carry):
        base = pl.multiple_of(i * GLA_SUB, GLA_SUB)
        prev = jnp.maximum(base - 1, 0)
        for h in range(H):
            b_start = b_ref[h, pl.ds(prev, 1), :]
            b_i = b_ref[h, pl.ds(base, GLA_SUB), :]
            qs_i = qs_ref[h, pl.ds(base, GLA_SUB), :]
            kx = kf_ref[h] * jnp.exp(jnp.minimum(b_start - b_ref[h], 0.0))
            qx = qs_i * jnp.exp(jnp.minimum(b_i - b_start, 0.0))
            s = _dot_nt(qx.astype(BF16), kx.astype(BF16))
            s = jnp.where(key_idx < base, s, 0.0)
            o_past = jnp.dot(s.astype(BF16), v_ref[:, h * dv:(h + 1) * dv], preferred_element_type=F32)
            terms = []
            for j in range(GLA_SUB):
                b_j = b_ref[h, pl.ds(base + j, 1), :]
                k_j = kf_ref[h, pl.ds(base + j, 1), :]
                v_j = vf_ref[h, pl.ds(base + j, 1), :]
                t = jnp.exp(jnp.minimum(b_i - b_j, 0.0)) * (qs_i * k_j)
                t = jnp.where(sub_row >= j, t, 0.0)
                terms.append(jnp.sum(t, axis=-1, keepdims=True) * v_j)
            while len(terms) > 1:
                terms = [a + b for a, b in zip(terms[0::2], terms[1::2])]
            acc_ref[h, pl.ds(base, GLA_SUB), :] += terms[0] + o_past
        return carry

    lax.fori_loop(0, C // GLA_SUB, sub_block, 0)


def _gla(p, tail, wup, bias, ng, batch, seq, layer):
    C, H, dk, dv = GLA_CHUNK_ROWS, GLA_HEADS, GLA_DK, GLA_DV
    nc = seq // C
    rows = lambda b, c: b * nc + c
    q0, k0 = PROJ_OFF[SEG_AQ] // (H * dk), PROJ_OFF[SEG_AK] // (H * dk)
    v0, g0 = PROJ_OFF[SEG_AV] // (H * dv), PROJ_OFF[SEG_AGATE] // (H * dv)
    lr0 = TAIL_OFF[SEG_ALR] // LANES
    return pl.pallas_call(
        _gla_kernel,
        grid=(batch, nc),
        in_specs=[
            pl.BlockSpec((C, H * dk), lambda b, c: (rows(b, c), q0)),
            pl.BlockSpec((C, H * dk), lambda b, c: (rows(b, c), k0)),
            pl.BlockSpec((C, H * dv), lambda b, c: (rows(b, c), v0)),
            pl.BlockSpec((C, H * dv), lambda b, c: (rows(b, c), g0)),
            pl.BlockSpec((C, LANES), lambda b, c: (rows(b, c), lr0)),
            pl.BlockSpec((None, LANES, H * dk), lambda b, c: (layer, 0, 0)),
            pl.BlockSpec((None, 1, H * dk), lambda b, c: (layer, 0, 0)),
            pl.BlockSpec((None, 1, dv), lambda b, c: (layer, 0, 0)),
        ],
        out_specs=pl.BlockSpec((C, H * dv), lambda b, c: (rows(b, c), 0)),
        out_shape=jax.ShapeDtypeStruct((batch * seq, H * dv), BF16),
        scratch_shapes=[
            pltpu.VMEM((H, dk, dv), F32),
            pltpu.VMEM((H, C, dk), F32),
            pltpu.VMEM((H, C, dk), F32),
            pltpu.VMEM((H, C, dk), BF16),
            pltpu.VMEM((H, C, dk), F32),
            pltpu.VMEM((H, C, dv), F32),
            pltpu.VMEM((H, C, dv), F32),
            pltpu.VMEM((H, C, C), BF16),
        ],
        compiler_params=pltpu.CompilerParams(dimension_semantics=("parallel", "arbitrary")),
        name="gla",
    )(p, p, p, p, tail, wup, bias, ng)


SWA_BLOCKS_PER_STEP = 4
SWA_MASKED = -1e30


def _swa_kernel(layer, sink_ref, q_ref, kp_ref, kc_ref, vp_ref, vc_ref, gate_ref, o_ref, s_ref, p_ref):
    W, hd, NB = SWA_WINDOW, SWA_HEAD_DIM, SWA_BLOCKS_PER_STEP
    group = SWA_Q_HEADS // SWA_KV_HEADS
    pairs = group // 2
    NQ = group * W
    n = pl.program_id(1)
    kk = jnp.concatenate([kp_ref[...], kc_ref[...]], axis=0)
    vv = jnp.concatenate([vp_ref[...], vc_ref[...]], axis=0)
    vt_all = jnp.transpose(vv.astype(F32))
    key_lane = lax.broadcasted_iota(jnp.int32, kk.shape, 1)
    q_lane = lax.broadcasted_iota(jnp.int32, (W, LANES), 1)
    key = lax.broadcasted_iota(jnp.int32, (2 * W, W), 0)
    qry = lax.broadcasted_iota(jnp.int32, (2 * W, W), 1)
    band = (key > qry) & (key <= qry + W)
    head_of_col = lax.broadcasted_iota(jnp.int32, (1, NQ), 1) // W
    swap = lambda t: jnp.concatenate([t[:, hd:], t[:, :hd]], axis=1)
    low_half = lambda t, lane, first: (jnp.where(lane < hd, t, jnp.zeros_like(t)) if first
                                       else swap(jnp.where(lane >= hd, t, jnp.zeros_like(t))))
    c = (hd ** -0.5) * LOG2_E
    inv_scale = float(hd) ** 0.5
    k_low = [low_half(kk, key_lane, hk == 0) for hk in range(SWA_KV_HEADS)]
    for sub in range(NB):
        allowed = band & ((key >= W) | (n * NB + sub > 0))
        q_rows = slice(sub * W, (sub + 1) * W)
        for hk in range(SWA_KV_HEADS):
            slot = sub * SWA_KV_HEADS + hk
            k_lo = k_low[hk][sub * W:(sub + 2) * W]
            vt = vt_all[hk * hd:(hk + 1) * hd, sub * W:(sub + 2) * W].astype(BF16)
            q_stack = jnp.concatenate(
                [low_half(q_ref[q_rows, (hk * group + 2 * pr) * hd:(hk * group + 2 * pr + 2) * hd], q_lane, first)
                 for pr in range(pairs) for first in (True, False)], axis=0)
            sink = jnp.zeros((1, NQ), F32)
            for g in range(group):
                sink = jnp.where(head_of_col == g, sink_ref[layer, hk * group + g] * inv_scale, sink)
            m = sink
            for r in range(2):
                rows = slice(r * W, (r + 1) * W)
                sc = _dot_nt(k_lo[rows], q_stack)
                for g in range(group):
                    cols = slice(g * W, (g + 1) * W)
                    s_ref[slot, rows, cols] = jnp.where(allowed[rows], sc[:, cols], SWA_MASKED)
                m = jnp.maximum(m, jnp.max(s_ref[slot, rows, :], axis=0, keepdims=True))
            den = jnp.exp2((sink - m) * c)
            for r in range(2):
                rows = slice(r * W, (r + 1) * W)
                p = jnp.exp2((s_ref[slot, rows, :] - m) * c)
                den = den + jnp.sum(p, axis=0, keepdims=True)
                p_ref[slot, rows, :] = p.astype(BF16)
            o_t = jnp.dot(vt, p_ref[slot], preferred_element_type=F32) * (1.0 / den)
            for pr in range(pairs):
                col0 = (hk * group + 2 * pr) * hd
                o = jnp.transpose(jnp.concatenate([o_t[:, 2 * pr * W:(2 * pr + 1) * W],
                                                   o_t[:, (2 * pr + 1) * W:(2 * pr + 2) * W]], axis=0))
                gate = gate_ref[q_rows, col0:col0 + LANES].astype(F32)
                o_ref[q_rows, col0:col0 + LANES] = (o * _silu(gate)).astype(o_ref.dtype)


def _swa(p, tail, sinks, batch, seq, layer):
    W, NB = SWA_WINDOW, SWA_BLOCKS_PER_STEP
    steps = seq // (NB * W)
    assert seq % (NB * W) == 0
    width = SWA_Q_HEADS * SWA_HEAD_DIM
    q0, g0 = PROJ_OFF[SEG_BQ] // width, PROJ_OFF[SEG_BGATE] // width
    k0, v0 = TAIL_OFF[SEG_BK] // LANES, TAIL_OFF[SEG_BV] // LANES
    cur = lambda b, n: b * steps + n
    prev = lambda b, n: (b * steps + n) * NB - jnp.minimum(n, 1)
    slots = NB * SWA_KV_HEADS
    nq = SWA_Q_HEADS // SWA_KV_HEADS * W
    return pl.pallas_call(
        functools.partial(_swa_kernel, layer),
        grid=(batch, steps),
        in_specs=[
            pl.BlockSpec(memory_space=pltpu.SMEM),
            pl.BlockSpec((NB * W, width), lambda b, n: (cur(b, n), q0)),
            pl.BlockSpec((W, LANES), lambda b, n: (prev(b, n), k0)),
            pl.BlockSpec((NB * W, LANES), lambda b, n: (cur(b, n), k0)),
            pl.BlockSpec((W, LANES), lambda b, n: (prev(b, n), v0)),
            pl.BlockSpec((NB * W, LANES), lambda b, n: (cur(b, n), v0)),
            pl.BlockSpec((NB * W, width), lambda b, n: (cur(b, n), g0)),
        ],
        out_specs=pl.BlockSpec((NB * W, width), lambda b, n: (cur(b, n), 0)),
        out_shape=jax.ShapeDtypeStruct((batch * seq, width), BF16),
        scratch_shapes=[
            pltpu.VMEM((slots, 2 * W, nq), F32),
            pltpu.VMEM((slots, 2 * W, nq), BF16),
        ],
        compiler_params=pltpu.CompilerParams(dimension_semantics=("parallel", "parallel")),
        name="swa",
    )(sinks, p, tail, tail, tail, tail, p)


MOBA_PENALTY = 1e30
LOG2_E = 1.4426950408889634


MOBA_HEAD_GROUP = 4


MOBA_KEY_CHUNK = 128
MOBA_SUM_ROWS = 16


def _moba_kernel(q_ref, k_ref, v_ref, gate_ref, o_ref,
                 kmean_ref, vt_ref, pen_ref, acc_ref, s_ref, p_ref, so_ref, po_ref):
    BLK, hd, HG, RC = MOBA_BLOCK, MOBA_HEAD_DIM, MOBA_HEAD_GROUP, MOBA_KEY_CHUNK
    PAIR = 2 * BLK
    n_chunks = PAIR // RC
    i = pl.program_id(2)
    nblk = k_ref.shape[0] // BLK
    heads = [slice(h * hd, (h + 1) * hd) for h in range(HG)]
    c = (hd ** -0.5) * LOG2_E

    @pl.when(i == 0)
    def _():
        for h, cols in enumerate(heads):
            kf = k_ref[:, cols].astype(F32).reshape(nblk, BLK, hd)
            rest = jnp.sum(kf, axis=1) * (1.0 / BLK)
            rest = jnp.concatenate([rest, jnp.zeros((LANES - nblk, hd), F32)], axis=0)
            for t in range(3):
                term = rest.astype(BF16)
                kmean_ref[h, t * nblk:(t + 1) * nblk, :] = term[:nblk]
                rest = rest - term.astype(F32)
            extra = lax.broadcasted_iota(jnp.int32, (MOBA_SUM_ROWS, BLK), 0)
            ones_rows = jnp.where(extra == 0, 1.0, 0.0).astype(BF16)
            for n in range(nblk):
                vt = jnp.transpose(v_ref[n * BLK:(n + 1) * BLK, cols].astype(F32)).astype(BF16)
                vt_ref[h, n // 2, :, (n % 2) * BLK:(n % 2 + 1) * BLK] = jnp.concatenate([vt, ones_rows], axis=0)

    def softmax_steps(pair, carry, next_pair=None):
        m_news, alphas = [], []
        for h in range(HG):
            m_prev, m_a, m_b = carry[h]
            pen_a = pen_ref[h, pl.ds(2 * pair, 1), :]
            pen_b = pen_ref[h, pl.ds(2 * pair + 1, 1), :]
            m_new = jnp.maximum(m_prev, jnp.maximum(m_a - pen_a, m_b - pen_b))
            for r in range(n_chunks):
                off = m_new + (pen_a if r < n_chunks // 2 else pen_b)
                p = jnp.exp2((s_ref[h, r * RC:(r + 1) * RC, :] - off) * c)
                p_ref[h, r * RC:(r + 1) * RC, :] = p.astype(BF16)
            m_news.append(m_new)
            alphas.append(jnp.exp2((m_prev - m_new) * c))
        maxes = [()] * HG
        if next_pair is not None:
            start = pl.multiple_of(next_pair * PAIR, PAIR)
            for h in range(HG):
                sc = _dot_nt(k_ref[pl.ds(start, PAIR), heads[h]], q_ref[:, heads[h]])
                s_ref[h] = sc
                maxes[h] = (jnp.max(sc[:BLK], axis=0, keepdims=True), jnp.max(sc[BLK:], axis=0, keepdims=True))
        for h in range(HG):
            acc_ref[h] = alphas[h] * acc_ref[h] + jnp.dot(vt_ref[h, pair], p_ref[h],
                                                          preferred_element_type=F32)
        return tuple((m_news[h],) + maxes[h] for h in range(HG))

    own = pl.multiple_of(i * BLK, BLK)
    key_c = lax.broadcasted_iota(jnp.int32, (RC, BLK), 0)
    qry_c = lax.broadcasted_iota(jnp.int32, (RC, BLK), 1)
    blk = lax.broadcasted_iota(jnp.int32, (nblk, BLK), 0)
    neg_inf = jnp.float32(-jnp.inf)

    gates, own_maxes, pair0_maxes = [], [], []
    for h, cols in enumerate(heads):
        q = q_ref[:, cols]
        lhs = jnp.concatenate([k_ref[pl.ds(own, BLK), cols], kmean_ref[h], k_ref[0:PAIR, cols]], axis=0)
        both = _dot_nt(lhs, q)
        gate = both[BLK:BLK + nblk] + both[BLK + nblk:BLK + 2 * nblk] + both[BLK + 2 * nblk:BLK + 3 * nblk]
        pair0 = both[BLK + 3 * nblk:]
        s_ref[h] = pair0
        pair0_max = (jnp.max(pair0[:BLK], axis=0, keepdims=True), jnp.max(pair0[BLK:], axis=0, keepdims=True))
        own_max = []
        for r in range(BLK // RC):
            rows = slice(r * RC, (r + 1) * RC)
            sc = jnp.where(key_c + r * RC <= qry_c, both[rows], -MOBA_PENALTY)
            so_ref[h, rows, :] = sc
            own_max.append(jnp.max(sc, axis=0, keepdims=True))
        gates.append(gate)
        own_maxes.append(functools.reduce(jnp.maximum, own_max))
        pair0_maxes.append(pair0_max)

    carry = []
    for h, cols in enumerate(heads):
        g = jnp.where(blk < i, gates[h], neg_inf)
        sel = jnp.zeros(g.shape, jnp.bool_)
        for _ in range(MOBA_TOPK):
            m = jnp.max(g, axis=0, keepdims=True)
            idx = jnp.min(jnp.where(g == m, blk, nblk), axis=0, keepdims=True)
            pick = (blk == idx) & (m > neg_inf)
            sel = sel | pick
            g = jnp.where(pick, neg_inf, g)
        pen_ref[h] = jnp.where(sel, 0.0, MOBA_PENALTY)

        m0 = own_maxes[h]
        for r in range(BLK // RC):
            rows = slice(r * RC, (r + 1) * RC)
            po_ref[h, rows, :] = jnp.exp2((so_ref[h, rows, :] - m0) * c).astype(BF16)
        own_vt = vt_ref[h, i // 2, :, pl.ds(pl.multiple_of((i % 2) * BLK, BLK), BLK)]
        acc_ref[h] = jnp.dot(own_vt, po_ref[h], preferred_element_type=F32)
        carry.append((m0,) + pair0_maxes[h])

    n_pairs = (i + 1) // 2

    carry = lax.fori_loop(0, jnp.maximum(n_pairs - 1, 0),
                          lambda j, carry: softmax_steps(j, carry, next_pair=j + 1), tuple(carry))
    softmax_steps(jnp.maximum(n_pairs - 1, 0), carry)
    for h, cols in enumerate(heads):
        o = jnp.transpose(acc_ref[h, 0:hd, :] * (1.0 / acc_ref[h, hd:hd + 1, :]))
        o_ref[:, cols] = (o * _silu(gate_ref[:, cols].astype(F32))).astype(o_ref.dtype)


def _moba(p, batch, seq):
    BLK, hd, HG = MOBA_BLOCK, MOBA_HEAD_DIM, MOBA_HEAD_GROUP
    nblk = seq // BLK
    assert seq % (2 * BLK) == 0 and MOBA_HEADS % HG == 0
    gw = HG * hd
    q0, k0 = PROJ_OFF[SEG_CQ] // gw, PROJ_OFF[SEG_CK] // gw
    v0, g0 = PROJ_OFF[SEG_CV] // gw, PROJ_OFF[SEG_CGATE] // gw
    scratch_bytes = HG * (seq * hd * 2 + hd * BLK * 4 + 2 * BLK * BLK * 4 + 2 * BLK * BLK * 2)
    vmem = 2 * (2 * seq * gw * 2 + 3 * BLK * gw * 2) + scratch_bytes + (12 << 20)
    return pl.pallas_call(
        _moba_kernel,
        grid=(batch, MOBA_HEADS // HG, nblk),
        in_specs=[
            pl.BlockSpec((BLK, gw), lambda b, h, i: (b * nblk + i, q0 + h)),
            pl.BlockSpec((seq, gw), lambda b, h, i: (b, k0 + h)),
            pl.BlockSpec((seq, gw), lambda b, h, i: (b, v0 + h)),
            pl.BlockSpec((BLK, gw), lambda b, h, i: (b * nblk + i, g0 + h)),
        ],
        out_specs=pl.BlockSpec((BLK, gw), lambda b, h, i: (b * nblk + i, h)),
        out_shape=jax.ShapeDtypeStruct((batch * seq, MOBA_HEADS * hd), BF16),
        scratch_shapes=[
            pltpu.VMEM((HG, 3 * nblk, hd), BF16),
            pltpu.VMEM((HG, nblk // 2, hd + MOBA_SUM_ROWS, 2 * BLK), BF16),
            pltpu.VMEM((HG, nblk, BLK), F32),
            pltpu.VMEM((HG, hd + MOBA_SUM_ROWS, BLK), F32),
            pltpu.VMEM((HG, 2 * BLK, BLK), F32),
            pltpu.VMEM((HG, 2 * BLK, BLK), BF16),
            pltpu.VMEM((HG, BLK, BLK), F32),
            pltpu.VMEM((HG, BLK, BLK), BF16),
        ],
        compiler_params=pltpu.CompilerParams(
            dimension_semantics=("parallel", "parallel", "arbitrary"), vmem_limit_bytes=vmem),
        name="moba",
    )(p, p, p, p)


MERGE_TILE_M = 256


def _merge_kernel(alpha, ya_ref, yb_ref, yc_ref, mg_ref, x_ref, wb_ref, bm_ref, wo_ref, lg_ref, lb_ref,
                  y_ref, yb16_ref):
    D = D_MODEL
    merged = None
    for n, y_n in enumerate((ya_ref, yb_ref, yc_ref)):
        up = jnp.dot(y_n[...], wb_ref[n], preferred_element_type=F32)
        gate = _sigmoid(mg_ref[:, n * D:(n + 1) * D].astype(F32) + bm_ref[n:n + 1, :])
        merged = gate * up if merged is None else merged + gate * up
    out = jnp.dot(merged.astype(BF16), wo_ref[...], preferred_element_type=F32)
    h = alpha * x_ref[...] + out
    mu = jnp.mean(h, axis=-1, keepdims=True)
    hc = h - mu
    var = jnp.mean(hc * hc, axis=-1, keepdims=True)
    y = hc * lax.rsqrt(var + LN_EPS) * lg_ref[...] + lb_ref[...]
    y_ref[...] = y
    yb16_ref[...] = y.astype(BF16)


def _merge(ya, yb, yc, p, x, wb, bm, wo, lg, lb, alpha, layer):
    m = x.shape[0]
    D, Wd = D_MODEL, BRANCH_WIDTH
    tm = min(MERGE_TILE_M, m)
    once = pl.Buffered(1)
    resident = (N_BRANCH * Wd * D + D * D) * 2
    streamed = 2 * (3 * tm * Wd * 2 + tm * 3 * D * 2 + tm * D * 4 + tm * D * 4 + tm * D * 2)
    vmem = resident + streamed + (12 << 20)
    row = lambda i: (i, 0)
    return pl.pallas_call(
        functools.partial(_merge_kernel, alpha),
        grid=(m // tm,),
        in_specs=[
            pl.BlockSpec((tm, Wd), row), pl.BlockSpec((tm, Wd), row), pl.BlockSpec((tm, Wd), row),
            pl.BlockSpec((tm, N_BRANCH * D), lambda i: (i, PROJ_OFF[SEG_MGATE] // (N_BRANCH * D))),
            pl.BlockSpec((tm, D), row),
            pl.BlockSpec((N_BRANCH, Wd, D), lambda i: (0, 0, 0), pipeline_mode=once),
            pl.BlockSpec((None, N_BRANCH, D), lambda i: (layer, 0, 0)),
            pl.BlockSpec((D, D), lambda i: (0, 0), pipeline_mode=once),
            pl.BlockSpec((None, 1, D), lambda i: (layer, 0, 0)),
            pl.BlockSpec((None, 1, D), lambda i: (layer, 0, 0)),
        ],
        out_specs=[pl.BlockSpec((tm, D), row), pl.BlockSpec((tm, D), row)],
        out_shape=[jax.ShapeDtypeStruct((m, D), F32), jax.ShapeDtypeStruct((m, D), BF16)],
        compiler_params=pltpu.CompilerParams(
            dimension_semantics=("parallel",), vmem_limit_bytes=min(vmem, V7X_VMEM_BYTES - (4 << 20))),
        name="merge_out_ln",
    )(ya, yb, yc, p, x, wb, bm, wo, lg, lb)


def kernel(x, w_in, gla_w_up, gla_b, gla_norm_g, swa_sinks, b_merge, w_branch, w_o, ln_g, ln_b):
    batch, seq, d = x.shape
    depth = w_in.shape[0]
    alpha = (2 * depth) ** 0.25
    assert d == D_MODEL and seq % (2 * MOBA_BLOCK) == 0 and seq % GLA_CHUNK_ROWS == 0

    w_in_t = jnp.swapaxes(w_in, 1, 2)
    wup = jnp.pad(gla_w_up, ((0, 0), (0, LANES - GLA_RANK), (0, 0))).astype(BF16)
    w_branch_rows = w_branch.reshape(depth, N_BRANCH * BRANCH_WIDTH, d)
    gla_bias, gla_ng = gla_b[:, None, :], gla_norm_g[:, None, :]
    lg, lb = ln_g[:, None, :], ln_b[:, None, :]

    xf = x.reshape(batch * seq, d)
    xb = xf.astype(BF16)
    for l in range(depth):
        p, wb, wo = _proj_and_cast(xb, w_in_t, w_branch_rows, w_o, l)
        tail = _proj(xb, w_in_t, l, TAIL_TILE_SRC, TAIL_TILE_N, TAIL_TILE_M, "in_proj_tail")
        ya = _gla(p, tail, wup, gla_bias, gla_ng, batch, seq, l)
        yb = _swa(p, tail, swa_sinks, batch, seq, l)
        yc = _moba(p, batch, seq)
        xf, xb = _merge(ya, yb, yc, p, xf, wb.reshape(N_BRANCH, BRANCH_WIDTH, d), b_merge, wo, lg, lb, alpha, l)
    return xf.reshape(batch, seq, d)
```

```python
import functools

import jax
import jax.numpy as jnp
import numpy as np
from jax import lax
from jax.experimental import pallas as pl
from jax.experimental.pallas import tpu as pltpu

F32 = jnp.float32
BF16 = jnp.bfloat16

D_MODEL = 2048
BRANCH_WIDTH = 1024
N_BRANCH = 3
GLA_HEADS, GLA_DK, GLA_DV, GLA_RANK, GLA_TAU = 4, 128, 256, 16, 16.0
SWA_Q_HEADS, SWA_KV_HEADS, SWA_HEAD_DIM, SWA_WINDOW = 16, 2, 64, 128
MOBA_HEADS, MOBA_HEAD_DIM, MOBA_BLOCK, MOBA_TOPK = 8, 128, 256, 3
LN_EPS = 1e-5
RMS_EPS = 1e-6

IN_SPLITS = (
    GLA_HEADS * GLA_DK, GLA_HEADS * GLA_DK, GLA_HEADS * GLA_DV, GLA_RANK, BRANCH_WIDTH,
    SWA_Q_HEADS * SWA_HEAD_DIM, SWA_KV_HEADS * SWA_HEAD_DIM, SWA_KV_HEADS * SWA_HEAD_DIM, BRANCH_WIDTH,
    MOBA_HEADS * MOBA_HEAD_DIM, MOBA_HEADS * MOBA_HEAD_DIM, MOBA_HEADS * MOBA_HEAD_DIM, BRANCH_WIDTH,
    N_BRANCH * D_MODEL,
)
(SEG_AQ, SEG_AK, SEG_AV, SEG_ALR, SEG_AGATE, SEG_BQ, SEG_BK, SEG_BV, SEG_BGATE,
 SEG_CQ, SEG_CK, SEG_CV, SEG_CGATE, SEG_MGATE) = range(14)

LANES = 128
V7X_VMEM_BYTES = 64 * 1024 * 1024

PROJ_ORDER = (SEG_MGATE, SEG_AV, SEG_AGATE, SEG_BQ, SEG_BGATE, SEG_CQ, SEG_CK, SEG_CV, SEG_CGATE,
              SEG_AQ, SEG_AK)
PROJ_TILE_N = 1024
TAIL_TILE_N = 256
SRC_OFF = [int(v) for v in np.concatenate([[0], np.cumsum(IN_SPLITS)])]


def _proj_layout():
    offs, runs, cur = {}, [], 0
    for seg in PROJ_ORDER:
        offs[seg] = cur
        cur += IN_SPLITS[seg]
        if runs and runs[-1][0] + runs[-1][1] == SRC_OFF[seg]:
            runs[-1][1] += IN_SPLITS[seg]
        else:
            runs.append([SRC_OFF[seg], IN_SPLITS[seg]])
    tile_src = []
    for start, length in runs:
        assert length % PROJ_TILE_N == 0
        tile_src += [start + t for t in range(0, length, PROJ_TILE_N)]
    return offs, tile_src


PROJ_OFF, PROJ_TILE_SRC = _proj_layout()
assert SRC_OFF[SEG_BV] == SRC_OFF[SEG_BK] + IN_SPLITS[SEG_BK] and IN_SPLITS[SEG_BK] + IN_SPLITS[SEG_BV] == TAIL_TILE_N
TAIL_TILE_SRC = [SRC_OFF[SEG_BK], SRC_OFF[SEG_ALR]]
TAIL_OFF = {SEG_BK: 0, SEG_BV: IN_SPLITS[SEG_BK], SEG_ALR: TAIL_TILE_N}
PROJ_SRC_ALIGN = 16
assert all(v % PROJ_SRC_ALIGN == 0 for v in PROJ_TILE_SRC + TAIL_TILE_SRC)


def _silu(x):
    return x * (1.0 / (1.0 + jnp.exp(-x)))


def _sigmoid(x):
    return 1.0 / (1.0 + jnp.exp(-x))


def _dot_nt(a, b):
    return lax.dot_general(a, b, (((1,), (1,)), ((), ())), preferred_element_type=F32)


PROJ_TILE_M = 2048
TAIL_TILE_M = 2048


def _proj_kernel(src_ref, x_ref, wt_ref, o_ref):
    del src_ref
    o_ref[...] = _dot_nt(x_ref[...], wt_ref[0].astype(BF16)).astype(o_ref.dtype)


def _proj(xb, wt, layer, tile_src, tn, tile_m, name):
    m, d = xb.shape
    tm = min(tile_m, m)
    vmem = 2 * (tm * d * 2 + d * tn * 4 + tm * tn * 2) + d * tn * 2 + (8 << 20)
    grid_spec = pltpu.PrefetchScalarGridSpec(
        num_scalar_prefetch=1,
        grid=(m // tm, len(tile_src)),
        in_specs=[pl.BlockSpec((tm, d), lambda i, j, src: (i, 0)),
                  pl.BlockSpec((pl.Element(1), pl.Element(tn), pl.Element(d)),
                               lambda i, j, src: (layer, pl.multiple_of(src[j], PROJ_SRC_ALIGN), 0))],
        out_specs=pl.BlockSpec((tm, tn), lambda i, j, src: (i, j)),
    )
    return pl.pallas_call(
        _proj_kernel,
        grid_spec=grid_spec,
        out_shape=jax.ShapeDtypeStruct((m, len(tile_src) * tn), BF16),
        compiler_params=pltpu.CompilerParams(
            dimension_semantics=("parallel", "arbitrary"), vmem_limit_bytes=vmem),
        name=name,
    )(jnp.asarray(tile_src, jnp.int32), xb, wt)


def _proj_cast_kernel(src_ref, x_ref, wt_ref, wb_ref, wo_ref, o_ref, wb16_ref, wo16_ref):
    del src_ref
    o_ref[...] = _dot_nt(x_ref[...], wt_ref[0].astype(BF16)).astype(o_ref.dtype)
    wb16_ref[...] = wb_ref[...].astype(BF16)
    wo16_ref[...] = wo_ref[...].astype(BF16)


def _proj_and_cast(xb, wt, w_branch, w_o, layer):
    m, d = xb.shape
    tm, tn, tile_src = min(PROJ_TILE_M, m), PROJ_TILE_N, PROJ_TILE_SRC
    n_tiles = len(tile_src)
    rows_b, rows_o = w_branch.shape[1], w_o.shape[1]
    steps = (m // tm) * n_tiles
    slab = next(r for r in (16 << k for k in range(12))
                if rows_b % r == 0 and rows_o % r == 0 and steps * r >= max(rows_b, rows_o))
    slab_b = lambda i, j: jnp.minimum(i * n_tiles + j, rows_b // slab - 1)
    slab_o = lambda i, j: jnp.minimum(i * n_tiles + j, rows_o // slab - 1)
    vmem = (2 * (tm * d * 2 + d * tn * 4 + tm * tn * 2) + d * tn * 2 + 4 * 2 * slab * d * 6 + (8 << 20))
    grid_spec = pltpu.PrefetchScalarGridSpec(
        num_scalar_prefetch=1,
        grid=(m // tm, n_tiles),
        in_specs=[pl.BlockSpec((tm, d), lambda i, j, src: (i, 0)),
                  pl.BlockSpec((pl.Element(1), pl.Element(tn), pl.Element(d)),
                               lambda i, j, src: (layer, pl.multiple_of(src[j], PROJ_SRC_ALIGN), 0)),
                  pl.BlockSpec((None, slab, d), lambda i, j, src: (layer, slab_b(i, j), 0)),
                  pl.BlockSpec((None, slab, d), lambda i, j, src: (layer, slab_o(i, j), 0))],
        out_specs=[pl.BlockSpec((tm, tn), lambda i, j, src: (i, j)),
                   pl.BlockSpec((slab, d), lambda i, j, src: (slab_b(i, j), 0)),
                   pl.BlockSpec((slab, d), lambda i, j, src: (slab_o(i, j), 0))],
    )
    return pl.pallas_call(
        _proj_cast_kernel,
        grid_spec=grid_spec,
        out_shape=[jax.ShapeDtypeStruct((m, n_tiles * tn), BF16),
                   jax.ShapeDtypeStruct((rows_b, d), BF16), jax.ShapeDtypeStruct((rows_o, d), BF16)],
        compiler_params=pltpu.CompilerParams(
            dimension_semantics=("arbitrary", "arbitrary"), vmem_limit_bytes=vmem),
        name="in_proj",
    )(jnp.asarray(tile_src, jnp.int32), xb, wt, w_branch, w_o)


GLA_CHUNK_ROWS = 256
GLA_SUB = 16
GLA_FAST_MAX_DECAY = 30.0
GLA_FAST_MIN_Q = 1e-20
GLA_FAST_MAX_QK = 1e20


def _gla_kernel(q_ref, k_ref, v_ref, gate_ref, alr_ref, wup_ref, bias_ref, ng_ref, o_ref,
                s_ref, b_ref, qs_ref, qx_ref, kf_ref, vf_ref, acc_ref, a_ref):
    C, H, dk, dv = GLA_CHUNK_ROWS, GLA_HEADS, GLA_DK, GLA_DV

    @pl.when(pl.program_id(1) == 0)
    def _():
        s_ref[...] = jnp.zeros_like(s_ref)

    rank_lane = lax.broadcasted_iota(jnp.int32, alr_ref.shape, 1) < GLA_RANK
    alr = jnp.where(rank_lane, alr_ref[...], jnp.zeros_like(alr_ref))
    z = jnp.dot(alr, wup_ref[...], preferred_element_type=F32) + bias_ref[...]
    g = -(jnp.maximum(-z, 0.0) + jnp.log1p(jnp.exp(-jnp.abs(z)))) * (1.0 / GLA_TAU)

    row = lax.broadcasted_iota(jnp.int32, (C, C), 0)
    col = lax.broadcasted_iota(jnp.int32, (C, C), 1)
    tril = jnp.where(col <= row, 1.0, 0.0).astype(BF16)
    g1 = g.astype(BF16)
    r1 = g - g1.astype(F32)
    g2 = r1.astype(BF16)
    g3 = (r1 - g2.astype(F32)).astype(BF16)
    b_all = (jnp.dot(tril, g1, preferred_element_type=F32)
             + jnp.dot(tril, g2, preferred_element_type=F32)
             + jnp.dot(tril, g3, preferred_element_type=F32))

    q_max = jnp.float32(0.0)
    k_max = jnp.float32(0.0)
    for h in range(H):
        b = b_all[:, h * dk:(h + 1) * dk]
        qs = q_ref[:, h * dk:(h + 1) * dk].astype(F32) * (dk ** -0.5)
        kf = k_ref[:, h * dk:(h + 1) * dk].astype(F32)
        qx = (qs * jnp.exp(b)).astype(BF16)
        b_ref[h] = b
        qs_ref[h] = qs
        kf_ref[h] = kf
        qx_ref[h] = qx
        q_max = jnp.maximum(q_max, jnp.max(jnp.abs(qs)))
        k_max = jnp.maximum(k_max, jnp.max(jnp.abs(kf)))
        acc_ref[h] = jnp.dot(qx, s_ref[h].astype(BF16), preferred_element_type=F32)

    total_decay = jnp.max(-b_all[C - 1:C, :])
    in_range = ((total_decay <= GLA_FAST_MAX_DECAY) & (q_max >= GLA_FAST_MIN_Q)
                & (q_max <= GLA_FAST_MAX_QK) & (k_max <= GLA_FAST_MAX_QK))

    @pl.when(in_range)
    def _():
        for h in range(H):
            k_inv = (kf_ref[h] * jnp.exp(-b_ref[h])).astype(BF16)
            a_ref[h] = jnp.where(col <= row, _dot_nt(qx_ref[h], k_inv), 0.0).astype(BF16)
        for h in range(H):
            acc_ref[h] += jnp.dot(a_ref[h], v_ref[:, h * dv:(h + 1) * dv], preferred_element_type=F32)

    @pl.when(jnp.logical_not(in_range))
    def _():
        _gla_sub_blocks(v_ref, b_ref, qs_ref, kf_ref, vf_ref, acc_ref)

    for h in range(H):
        b = b_ref[h]
        b_end = b[C - 1:C, :]
        k_end = kf_ref[h] * jnp.exp(b_end - b)
        decay_col = jnp.transpose(jnp.broadcast_to(jnp.exp(b_end), (dk, dk)))
        decay = jnp.concatenate([decay_col] * (dv // dk), axis=1)
        s_ref[h] = s_ref[h] * decay + jnp.dot(jnp.transpose(k_end).astype(BF16),
                                              v_ref[:, h * dv:(h + 1) * dv],
                                              preferred_element_type=F32)
        o = acc_ref[h]
        o = o * lax.rsqrt(jnp.mean(o * o, axis=-1, keepdims=True) + RMS_EPS) * ng_ref[...]
        gate = gate_ref[:, h * dv:(h + 1) * dv].astype(F32)
        o_ref[:, h * dv:(h + 1) * dv] = (o * _silu(gate)).astype(o_ref.dtype)


def _gla_sub_blocks(v_ref, b_ref, qs_ref, kf_ref, vf_ref, acc_ref):
    C, H, dk, dv = GLA_CHUNK_ROWS, GLA_HEADS, GLA_DK, GLA_DV
    for h in range(H):
        vf_ref[h] = v_ref[:, h * dv:(h + 1) * dv].astype(F32)
    key_idx = lax.broadcasted_iota(jnp.int32, (GLA_SUB, C), 1)
    sub_row = lax.broadcasted_iota(jnp.int32, (GLA_SUB, dk), 0)

    def sub_block(i, carry):
        base = pl.multiple_of(i * GLA_SUB, GLA_SUB)
        prev = jnp.maximum(base - 1, 0)
        for h in range(H):
            b_start = b_ref[h, pl.ds(prev, 1), :]
            b_i = b_ref[h, pl.ds(base, GLA_SUB), :]
            qs_i = qs_ref[h, pl.ds(base, GLA_SUB), :]
            kx = kf_ref[h] * jnp.exp(jnp.minimum(b_start - b_ref[h], 0.0))
            qx = qs_i * jnp.exp(jnp.minimum(b_i - b_start, 0.0))
            s = _dot_nt(qx.astype(BF16), kx.astype(BF16))
            s = jnp.where(key_idx < base, s, 0.0)
            o_past = jnp.dot(s.astype(BF16), v_ref[:, h * dv:(h + 1) * dv], preferred_element_type=F32)
            terms = []
            for j in range(GLA_SUB):
                b_j = b_ref[h, pl.ds(base + j, 1), :]
                k_j = kf_ref[h, pl.ds(base + j, 1), :]
                v_j = vf_ref[h, pl.ds(base + j, 1), :]
                t = jnp.exp(jnp.minimum(b_i - b_j, 0.0)) * (qs_i * k_j)
                t = jnp.where(sub_row >= j, t, 0.0)
                terms.append(jnp.sum(t, axis=-1, keepdims=True) * v_j)
            while len(terms) > 1:
                terms = [a + b for a, b in zip(terms[0::2], terms[1::2])]
            acc_ref[h, pl.ds(base, GLA_SUB), :] += terms[0] + o_past
        return carry

    lax.fori_loop(0, C // GLA_SUB, sub_block, 0)


def _gla(p, tail, wup, bias, ng, batch, seq, layer):
    C, H, dk, dv = GLA_CHUNK_ROWS, GLA_HEADS, GLA_DK, GLA_DV
    nc = seq // C
    rows = lambda b, c: b * nc + c
    q0, k0 = PROJ_OFF[SEG_AQ] // (H * dk), PROJ_OFF[SEG_AK] // (H * dk)
    v0, g0 = PROJ_OFF[SEG_AV] // (H * dv), PROJ_OFF[SEG_AGATE] // (H * dv)
    lr0 = TAIL_OFF[SEG_ALR] // LANES
    return pl.pallas_call(
        _gla_kernel,
        grid=(batch, nc),
        in_specs=[
            pl.BlockSpec((C, H * dk), lambda b, c: (rows(b, c), q0)),
            pl.BlockSpec((C, H * dk), lambda b, c: (rows(b, c), k0)),
            pl.BlockSpec((C, H * dv), lambda b, c: (rows(b, c), v0)),
            pl.BlockSpec((C, H * dv), lambda b, c: (rows(b, c), g0)),
            pl.BlockSpec((C, LANES), lambda b, c: (rows(b, c), lr0)),
            pl.BlockSpec((None, LANES, H * dk), lambda b, c: (layer, 0, 0)),
            pl.BlockSpec((None, 1, H * dk), lambda b, c: (layer, 0, 0)),
            pl.BlockSpec((None, 1, dv), lambda b, c: (layer, 0, 0)),
        ],
        out_specs=pl.BlockSpec((C, H * dv), lambda b, c: (rows(b, c), 0)),
        out_shape=jax.ShapeDtypeStruct((batch * seq, H * dv), BF16),
        scratch_shapes=[
            pltpu.VMEM((H, dk, dv), F32),
            pltpu.VMEM((H, C, dk), F32),
            pltpu.VMEM((H, C, dk), F32),
            pltpu.VMEM((H, C, dk), BF16),
            pltpu.VMEM((H, C, dk), F32),
            pltpu.VMEM((H, C, dv), F32),
            pltpu.VMEM((H, C, dv), F32),
            pltpu.VMEM((H, C, C), BF16),
        ],
        compiler_params=pltpu.CompilerParams(dimension_semantics=("parallel", "arbitrary")),
        name="gla",
    )(p, p, p, p, tail, wup, bias, ng)


SWA_BLOCKS_PER_STEP = 4
SWA_MASKED = -1e30


def _swa_kernel(layer, sink_ref, q_ref, kp_ref, kc_ref, vp_ref, vc_ref, gate_ref, o_ref, s_ref, p_ref):
    W, hd, NB = SWA_WINDOW, SWA_HEAD_DIM, SWA_BLOCKS_PER_STEP
    group = SWA_Q_HEADS // SWA_KV_HEADS
    pairs = group // 2
    NQ = group * W
    n = pl.program_id(1)
    kk = jnp.concatenate([kp_ref[...], kc_ref[...]], axis=0)
    vv = jnp.concatenate([vp_ref[...], vc_ref[...]], axis=0)
    vt_all = jnp.transpose(vv.astype(F32))
    key_lane = lax.broadcasted_iota(jnp.int32, kk.shape, 1)
    q_lane = lax.broadcasted_iota(jnp.int32, (W, LANES), 1)
    key = lax.broadcasted_iota(jnp.int32, (2 * W, W), 0)
    qry = lax.broadcasted_iota(jnp.int32, (2 * W, W), 1)
    band = (key > qry) & (key <= qry + W)
    head_of_col = lax.broadcasted_iota(jnp.int32, (1, NQ), 1) // W
    swap = lambda t: jnp.concatenate([t[:, hd:], t[:, :hd]], axis=1)
    low_half = lambda t, lane, first: (jnp.where(lane < hd, t, jnp.zeros_like(t)) if first
                                       else swap(jnp.where(lane >= hd, t, jnp.zeros_like(t))))
    c = (hd ** -0.5) * LOG2_E
    inv_scale = float(hd) ** 0.5
    k_low = [low_half(kk, key_lane, hk == 0) for hk in range(SWA_KV_HEADS)]
    for sub in range(NB):
        allowed = band & ((key >= W) | (n * NB + sub > 0))
        q_rows = slice(sub * W, (sub + 1) * W)
        for hk in range(SWA_KV_HEADS):
            slot = sub * SWA_KV_HEADS + hk
            k_lo = k_low[hk][sub * W:(sub + 2) * W]
            vt = vt_all[hk * hd:(hk + 1) * hd, sub * W:(sub + 2) * W].astype(BF16)
            q_stack = jnp.concatenate(
                [low_half(q_ref[q_rows, (hk * group + 2 * pr) * hd:(hk * group + 2 * pr + 2) * hd], q_lane, first)
                 for pr in range(pairs) for first in (True, False)], axis=0)
            sink = jnp.zeros((1, NQ), F32)
            for g in range(group):
                sink = jnp.where(head_of_col == g, sink_ref[layer, hk * group + g] * inv_scale, sink)
            m = sink
            for r in range(2):
                rows = slice(r * W, (r + 1) * W)
                sc = _dot_nt(k_lo[rows], q_stack)
                for g in range(group):
                    cols = slice(g * W, (g + 1) * W)
                    s_ref[slot, rows, cols] = jnp.where(allowed[rows], sc[:, cols], SWA_MASKED)
                m = jnp.maximum(m, jnp.max(s_ref[slot, rows, :], axis=0, keepdims=True))
            den = jnp.exp2((sink - m) * c)
            for r in range(2):
                rows = slice(r * W, (r + 1) * W)
                p = jnp.exp2((s_ref[slot, rows, :] - m) * c)
                den = den + jnp.sum(p, axis=0, keepdims=True)
                p_ref[slot, rows, :] = p.astype(BF16)
            o_t = jnp.dot(vt, p_ref[slot], preferred_element_type=F32) * (1.0 / den)
            for pr in range(pairs):
                col0 = (hk * group + 2 * pr) * hd
                o = jnp.transpose(jnp.concatenate([o_t[:, 2 * pr * W:(2 * pr + 1) * W],
                                                   o_t[:, (2 * pr + 1) * W:(2 * pr + 2) * W]], axis=0))
                gate = gate_ref[q_rows, col0:col0 + LANES].astype(F32)
                o_ref[q_rows, col0:col0 + LANES] = (o * _silu(gate)).astype(o_ref.dtype)


def _swa(p, tail, sinks, batch, seq, layer):
    W, NB = SWA_WINDOW, SWA_BLOCKS_PER_STEP
    steps = seq // (NB * W)
    assert seq % (NB * W) == 0
    width = SWA_Q_HEADS * SWA_HEAD_DIM
    q0, g0 = PROJ_OFF[SEG_BQ] // width, PROJ_OFF[SEG_BGATE] // width
    k0, v0 = TAIL_OFF[SEG_BK] // LANES, TAIL_OFF[SEG_BV] // LANES
    cur = lambda b, n: b * steps + n
    prev = lambda b, n: (b * steps + n) * NB - jnp.minimum(n, 1)
    slots = NB * SWA_KV_HEADS
    nq = SWA_Q_HEADS // SWA_KV_HEADS * W
    return pl.pallas_call(
        functools.partial(_swa_kernel, layer),
        grid=(batch, steps),
        in_specs=[
            pl.BlockSpec(memory_space=pltpu.SMEM),
            pl.BlockSpec((NB * W, width), lambda b, n: (cur(b, n), q0)),
            pl.BlockSpec((W, LANES), lambda b, n: (prev(b, n), k0)),
            pl.BlockSpec((NB * W, LANES), lambda b, n: (cur(b, n), k0)),
            pl.BlockSpec((W, LANES), lambda b, n: (prev(b, n), v0)),
            pl.BlockSpec((NB * W, LANES), lambda b, n: (cur(b, n), v0)),
            pl.BlockSpec((NB * W, width), lambda b, n: (cur(b, n), g0)),
        ],
        out_specs=pl.BlockSpec((NB * W, width), lambda b, n: (cur(b, n), 0)),
        out_shape=jax.ShapeDtypeStruct((batch * seq, width), BF16),
        scratch_shapes=[
            pltpu.VMEM((slots, 2 * W, nq), F32),
            pltpu.VMEM((slots, 2 * W, nq), BF16),
        ],
        compiler_params=pltpu.CompilerParams(dimension_semantics=("parallel", "parallel")),
        name="swa",
    )(sinks, p, tail, tail, tail, tail, p)


MOBA_PENALTY = 1e30
LOG2_E = 1.4426950408889634


MOBA_HEAD_GROUP = 4


MOBA_KEY_CHUNK = 128
MOBA_SUM_ROWS = 16


def _moba_kernel(q_ref, k0_ref, k1_ref, k2_ref, k3_ref, v0_ref, v1_ref, v2_ref, v3_ref, gate_ref, o_ref,
                 kmean_ref, vt_ref, pen_ref, acc_ref, s_ref, p_ref, so_ref, po_ref):
    BLK, hd, HG, RC = MOBA_BLOCK, MOBA_HEAD_DIM, MOBA_HEAD_GROUP, MOBA_KEY_CHUNK
    PAIR = 2 * BLK
    n_chunks = PAIR // RC
    i = pl.program_id(2)
    k_refs = (k0_ref, k1_ref, k2_ref, k3_ref)
    v_refs = (v0_ref, v1_ref, v2_ref, v3_ref)
    assert HG == len(k_refs)
    nblk = k0_ref.shape[0] // BLK
    heads = [slice(h * hd, (h + 1) * hd) for h in range(HG)]
    c = (hd ** -0.5) * LOG2_E

    @pl.when(i == 0)
    def _():
        for h, cols in enumerate(heads):
            kf = k_refs[h][...].astype(F32).reshape(nblk, BLK, hd)
            rest = jnp.sum(kf, axis=1) * (1.0 / BLK)
            rest = jnp.concatenate([rest, jnp.zeros((LANES - nblk, hd), F32)], axis=0)
            for t in range(3):
                term = rest.astype(BF16)
                kmean_ref[h, t * nblk:(t + 1) * nblk, :] = term[:nblk]
                rest = rest - term.astype(F32)
            extra = lax.broadcasted_iota(jnp.int32, (MOBA_SUM_ROWS, BLK), 0)
            ones_rows = jnp.where(extra == 0, 1.0, 0.0).astype(BF16)
            for n in range(nblk):
                vt = jnp.transpose(v_refs[h][n * BLK:(n + 1) * BLK, :].astype(F32)).astype(BF16)
                vt_ref[h, n // 2, :, (n % 2) * BLK:(n % 2 + 1) * BLK] = jnp.concatenate([vt, ones_rows], axis=0)

    def softmax_steps(pair, carry, next_pair=None):
        m_news, alphas = [], []
        for h in range(HG):
            m_prev, m_a, m_b = carry[h]
            pen_a = pen_ref[h, pl.ds(2 * pair, 1), :]
            pen_b = pen_ref[h, pl.ds(2 * pair + 1, 1), :]
            m_new = jnp.maximum(m_prev, jnp.maximum(m_a - pen_a, m_b - pen_b))
            for r in range(n_chunks):
                off = m_new + (pen_a if r < n_chunks // 2 else pen_b)
                p = jnp.exp2((s_ref[h, r * RC:(r + 1) * RC, :] - off) * c)
                p_ref[h, r * RC:(r + 1) * RC, :] = p.astype(BF16)
            m_news.append(m_new)
            alphas.append(jnp.exp2((m_prev - m_new) * c))
        maxes = [()] * HG
        if next_pair is not None:
            start = pl.multiple_of(next_pair * PAIR, PAIR)
            for h in range(HG):
                sc = _dot_nt(k_refs[h][pl.ds(start, PAIR), :], q_ref[:, heads[h]])
                s_ref[h] = sc
                maxes[h] = (jnp.max(sc[:BLK], axis=0, keepdims=True), jnp.max(sc[BLK:], axis=0, keepdims=True))
        for h in range(HG):
            acc_ref[h] = alphas[h] * acc_ref[h] + jnp.dot(vt_ref[h, pair], p_ref[h],
                                                          preferred_element_type=F32)
        return tuple((m_news[h],) + maxes[h] for h in range(HG))

    own = pl.multiple_of(i * BLK, BLK)
    key_c = lax.broadcasted_iota(jnp.int32, (RC, BLK), 0)
    qry_c = lax.broadcasted_iota(jnp.int32, (RC, BLK), 1)
    blk = lax.broadcasted_iota(jnp.int32, (nblk, BLK), 0)
    neg_inf = jnp.float32(-jnp.inf)

    gates, own_maxes, pair0_maxes = [], [], []
    for h, cols in enumerate(heads):
        q = q_ref[:, cols]
        lhs = jnp.concatenate([k_refs[h][pl.ds(own, BLK), :], kmean_ref[h], k_refs[h][0:PAIR, :]], axis=0)
        both = _dot_nt(lhs, q)
        gate = both[BLK:BLK + nblk] + both[BLK + nblk:BLK + 2 * nblk] + both[BLK + 2 * nblk:BLK + 3 * nblk]
        pair0 = both[BLK + 3 * nblk:]
        s_ref[h] = pair0
        pair0_max = (jnp.max(pair0[:BLK], axis=0, keepdims=True), jnp.max(pair0[BLK:], axis=0, keepdims=True))
        own_max = []
        for r in range(BLK // RC):
            rows = slice(r * RC, (r + 1) * RC)
            sc = jnp.where(key_c + r * RC <= qry_c, both[rows], -MOBA_PENALTY)
            so_ref[h, rows, :] = sc
            own_max.append(jnp.max(sc, axis=0, keepdims=True))
        gates.append(gate)
        own_maxes.append(functools.reduce(jnp.maximum, own_max))
        pair0_maxes.append(pair0_max)

    carry = []
    for h, cols in enumerate(heads):
        g = jnp.where(blk < i, gates[h], neg_inf)
        sel = jnp.zeros(g.shape, jnp.bool_)
        for _ in range(MOBA_TOPK):
            m = jnp.max(g, axis=0, keepdims=True)
            idx = jnp.min(jnp.where(g == m, blk, nblk), axis=0, keepdims=True)
            pick = (blk == idx) & (m > neg_inf)
            sel = sel | pick
            g = jnp.where(pick, neg_inf, g)
        pen_ref[h] = jnp.where(sel, 0.0, MOBA_PENALTY)

        m0 = own_maxes[h]
        for r in range(BLK // RC):
            rows = slice(r * RC, (r + 1) * RC)
            po_ref[h, rows, :] = jnp.exp2((so_ref[h, rows, :] - m0) * c).astype(BF16)
        own_vt = vt_ref[h, i // 2, :, pl.ds(pl.multiple_of((i % 2) * BLK, BLK), BLK)]
        acc_ref[h] = jnp.dot(own_vt, po_ref[h], preferred_element_type=F32)
        carry.append((m0,) + pair0_maxes[h])

    n_pairs = (i + 1) // 2

    carry = lax.fori_loop(0, jnp.maximum(n_pairs - 1, 0),
                          lambda j, carry: softmax_steps(j, carry, next_pair=j + 1), tuple(carry))
    softmax_steps(jnp.maximum(n_pairs - 1, 0), carry)
    for h, cols in enumerate(heads):
        o = jnp.transpose(acc_ref[h, 0:hd, :] * (1.0 / acc_ref[h, hd:hd + 1, :]))
        o_ref[:, cols] = (o * _silu(gate_ref[:, cols].astype(F32))).astype(o_ref.dtype)


def _moba(p, batch, seq):
    BLK, hd, HG = MOBA_BLOCK, MOBA_HEAD_DIM, MOBA_HEAD_GROUP
    nblk = seq // BLK
    assert seq % (2 * BLK) == 0 and MOBA_HEADS % HG == 0
    gw = HG * hd
    q0, k0 = PROJ_OFF[SEG_CQ] // gw, PROJ_OFF[SEG_CK] // gw
    v0, g0 = PROJ_OFF[SEG_CV] // gw, PROJ_OFF[SEG_CGATE] // gw
    scratch_bytes = HG * (seq * hd * 2 + hd * BLK * 4 + 2 * BLK * BLK * 4 + 2 * BLK * BLK * 2)
    vmem = 2 * (2 * seq * gw * 2 + 3 * BLK * gw * 2) + scratch_bytes + (12 << 20)
    return pl.pallas_call(
        _moba_kernel,
        grid=(batch, MOBA_HEADS // HG, nblk),
        in_specs=[
            pl.BlockSpec((BLK, gw), lambda b, h, i: (b * nblk + i, q0 + h)),
            *[pl.BlockSpec((seq, hd), functools.partial(lambda b, h, i, n: (b, (k0 + h) * HG + n), n=n))
              for n in range(HG)],
            *[pl.BlockSpec((seq, hd), functools.partial(lambda b, h, i, n: (b, (v0 + h) * HG + n), n=n))
              for n in range(HG)],
            pl.BlockSpec((BLK, gw), lambda b, h, i: (b * nblk + i, g0 + h)),
        ],
        out_specs=pl.BlockSpec((BLK, gw), lambda b, h, i: (b * nblk + i, h)),
        out_shape=jax.ShapeDtypeStruct((batch * seq, MOBA_HEADS * hd), BF16),
        scratch_shapes=[
            pltpu.VMEM((HG, 3 * nblk, hd), BF16),
            pltpu.VMEM((HG, nblk // 2, hd + MOBA_SUM_ROWS, 2 * BLK), BF16),
            pltpu.VMEM((HG, nblk, BLK), F32),
            pltpu.VMEM((HG, hd + MOBA_SUM_ROWS, BLK), F32),
            pltpu.VMEM((HG, 2 * BLK, BLK), F32),
            pltpu.VMEM((HG, 2 * BLK, BLK), BF16),
            pltpu.VMEM((HG, BLK, BLK), F32),
            pltpu.VMEM((HG, BLK, BLK), BF16),
        ],
        compiler_params=pltpu.CompilerParams(
            dimension_semantics=("parallel", "parallel", "arbitrary"), vmem_limit_bytes=vmem),
        name="moba",
    )(p, *([p] * (2 * HG)), p)


MERGE_TILE_M = 256


def _merge_kernel(alpha, ya_ref, yb_ref, yc_ref, mg_ref, x_ref, wb_ref, bm_ref, wo_ref, lg_ref, lb_ref,
                  y_ref, yb16_ref):
    D = D_MODEL
    merged = None
    for n, y_n in enumerate((ya_ref, yb_ref, yc_ref)):
        up = jnp.dot(y_n[...], wb_ref[n], preferred_element_type=F32)
        gate = _sigmoid(mg_ref[:, n * D:(n + 1) * D].astype(F32) + bm_ref[n:n + 1, :])
        merged = gate * up if merged is None else merged + gate * up
    out = jnp.dot(merged.astype(BF16), wo_ref[...], preferred_element_type=F32)
    h = alpha * x_ref[...] + out
    mu = jnp.mean(h, axis=-1, keepdims=True)
    hc = h - mu
    var = jnp.mean(hc * hc, axis=-1, keepdims=True)
    y = hc * lax.rsqrt(var + LN_EPS) * lg_ref[...] + lb_ref[...]
    y_ref[...] = y
    yb16_ref[...] = y.astype(BF16)


def _merge(ya, yb, yc, p, x, wb, bm, wo, lg, lb, alpha, layer):
    m = x.shape[0]
    D, Wd = D_MODEL, BRANCH_WIDTH
    tm = min(MERGE_TILE_M, m)
    once = pl.Buffered(1)
    resident = (N_BRANCH * Wd * D + D * D) * 2
    streamed = 2 * (3 * tm * Wd * 2 + tm * 3 * D * 2 + tm * D * 4 + tm * D * 4 + tm * D * 2)
    vmem = resident + streamed + (12 << 20)
    row = lambda i: (i, 0)
    return pl.pallas_call(
        functools.partial(_merge_kernel, alpha),
        grid=(m // tm,),
        in_specs=[
            pl.BlockSpec((tm, Wd), row), pl.BlockSpec((tm, Wd), row), pl.BlockSpec((tm, Wd), row),
            pl.BlockSpec((tm, N_BRANCH * D), lambda i: (i, PROJ_OFF[SEG_MGATE] // (N_BRANCH * D))),
            pl.BlockSpec((tm, D), row),
            pl.BlockSpec((N_BRANCH, Wd, D), lambda i: (0, 0, 0), pipeline_mode=once),
            pl.BlockSpec((None, N_BRANCH, D), lambda i: (layer, 0, 0)),
            pl.BlockSpec((D, D), lambda i: (0, 0), pipeline_mode=once),
            pl.BlockSpec((None, 1, D), lambda i: (layer, 0, 0)),
            pl.BlockSpec((None, 1, D), lambda i: (layer, 0, 0)),
        ],
        out_specs=[pl.BlockSpec((tm, D), row), pl.BlockSpec((tm, D), row)],
        out_shape=[jax.ShapeDtypeStruct((m, D), F32), jax.ShapeDtypeStruct((m, D), BF16)],
        compiler_params=pltpu.CompilerParams(
            dimension_semantics=("parallel",), vmem_limit_bytes=min(vmem, V7X_VMEM_BYTES - (4 << 20))),
        name="merge_out_ln",
    )(ya, yb, yc, p, x, wb, bm, wo, lg, lb)


def kernel(x, w_in, gla_w_up, gla_b, gla_norm_g, swa_sinks, b_merge, w_branch, w_o, ln_g, ln_b):
    batch, seq, d = x.shape
    depth = w_in.shape[0]
    alpha = (2 * depth) ** 0.25
    assert d == D_MODEL and seq % (2 * MOBA_BLOCK) == 0 and seq % GLA_CHUNK_ROWS == 0

    w_in_t = jnp.swapaxes(w_in, 1, 2)
    wup = jnp.pad(gla_w_up, ((0, 0), (0, LANES - GLA_RANK), (0, 0))).astype(BF16)
    w_branch_rows = w_branch.reshape(depth, N_BRANCH * BRANCH_WIDTH, d)
    gla_bias, gla_ng = gla_b[:, None, :], gla_norm_g[:, None, :]
    lg, lb = ln_g[:, None, :], ln_b[:, None, :]

    xf = x.reshape(batch * seq, d)
    xb = xf.astype(BF16)
    for l in range(depth):
        p, wb, wo = _proj_and_cast(xb, w_in_t, w_branch_rows, w_o, l)
        tail = _proj(xb, w_in_t, l, TAIL_TILE_SRC, TAIL_TILE_N, TAIL_TILE_M, "in_proj_tail")
        ya = _gla(p, tail, wup, gla_bias, gla_ng, batch, seq, l)
        yb = _swa(p, tail, swa_sinks, batch, seq, l)
        yc = _moba(p, batch, seq)
        xf, xb = _merge(ya, yb, yc, p, xf, wb.reshape(N_BRANCH, BRANCH_WIDTH, d), b_merge, wo, lg, lb, alpha, l)
    return xf.reshape(batch, seq, d)
```

```python
import functools

import jax
import jax.numpy as jnp
import numpy as np
from jax import lax
from jax.experimental import pallas as pl
from jax.experimental.pallas import tpu as pltpu

F32 = jnp.float32
BF16 = jnp.bfloat16

D_MODEL = 2048
BRANCH_WIDTH = 1024
N_BRANCH = 3
GLA_HEADS, GLA_DK, GLA_DV, GLA_RANK, GLA_TAU = 4, 128, 256, 16, 16.0
SWA_Q_HEADS, SWA_KV_HEADS, SWA_HEAD_DIM, SWA_WINDOW = 16, 2, 64, 128
MOBA_HEADS, MOBA_HEAD_DIM, MOBA_BLOCK, MOBA_TOPK = 8, 128, 256, 3
LN_EPS = 1e-5
RMS_EPS = 1e-6

IN_SPLITS = (
    GLA_HEADS * GLA_DK, GLA_HEADS * GLA_DK, GLA_HEADS * GLA_DV, GLA_RANK, BRANCH_WIDTH,
    SWA_Q_HEADS * SWA_HEAD_DIM, SWA_KV_HEADS * SWA_HEAD_DIM, SWA_KV_HEADS * SWA_HEAD_DIM, BRANCH_WIDTH,
    MOBA_HEADS * MOBA_HEAD_DIM, MOBA_HEADS * MOBA_HEAD_DIM, MOBA_HEADS * MOBA_HEAD_DIM, BRANCH_WIDTH,
    N_BRANCH * D_MODEL,
)
(SEG_AQ, SEG_AK, SEG_AV, SEG_ALR, SEG_AGATE, SEG_BQ, SEG_BK, SEG_BV, SEG_BGATE,
 SEG_CQ, SEG_CK, SEG_CV, SEG_CGATE, SEG_MGATE) = range(14)

LANES = 128
V7X_VMEM_BYTES = 64 * 1024 * 1024

PROJ_ORDER = (SEG_MGATE, SEG_AV, SEG_AGATE, SEG_BQ, SEG_BGATE, SEG_CQ, SEG_CK, SEG_CV, SEG_CGATE,
              SEG_AQ, SEG_AK)
PROJ_TILE_N = 1024
TAIL_TILE_N = 256
SRC_OFF = [int(v) for v in np.concatenate([[0], np.cumsum(IN_SPLITS)])]


def _proj_layout():
    offs, runs, cur = {}, [], 0
    for seg in PROJ_ORDER:
        offs[seg] = cur
        cur += IN_SPLITS[seg]
        if runs and runs[-1][0] + runs[-1][1] == SRC_OFF[seg]:
            runs[-1][1] += IN_SPLITS[seg]
        else:
            runs.append([SRC_OFF[seg], IN_SPLITS[seg]])
    tile_src = []
    for start, length in runs:
        assert length % PROJ_TILE_N == 0
        tile_src += [start + t for t in range(0, length, PROJ_TILE_N)]
    return offs, tile_src


PROJ_OFF, PROJ_TILE_SRC = _proj_layout()
assert SRC_OFF[SEG_BV] == SRC_OFF[SEG_BK] + IN_SPLITS[SEG_BK] and IN_SPLITS[SEG_BK] + IN_SPLITS[SEG_BV] == TAIL_TILE_N
TAIL_TILE_SRC = [SRC_OFF[SEG_BK]]
TAIL_OFF = {SEG_BK: 0, SEG_BV: IN_SPLITS[SEG_BK]}
PROJ_SRC_ALIGN = 16
assert all(v % PROJ_SRC_ALIGN == 0 for v in PROJ_TILE_SRC + TAIL_TILE_SRC)


def _silu(x):
    return x * (1.0 / (1.0 + jnp.exp(-x)))


def _sigmoid(x):
    return 1.0 / (1.0 + jnp.exp(-x))


def _dot_nt(a, b):
    return lax.dot_general(a, b, (((1,), (1,)), ((), ())), preferred_element_type=F32)


PROJ_TILE_M = 2048
TAIL_TILE_M = 2048


def _proj_kernel(src_ref, x_ref, wt_ref, o_ref):
    del src_ref
    o_ref[...] = _dot_nt(x_ref[...], wt_ref[0].astype(BF16)).astype(o_ref.dtype)


def _proj(xb, wt, layer, tile_src, tn, tile_m, name):
    m, d = xb.shape
    tm = min(tile_m, m)
    vmem = 2 * (tm * d * 2 + d * tn * 4 + tm * tn * 2) + d * tn * 2 + (8 << 20)
    grid_spec = pltpu.PrefetchScalarGridSpec(
        num_scalar_prefetch=1,
        grid=(m // tm, len(tile_src)),
        in_specs=[pl.BlockSpec((tm, d), lambda i, j, src: (i, 0)),
                  pl.BlockSpec((pl.Element(1), pl.Element(tn), pl.Element(d)),
                               lambda i, j, src: (layer, pl.multiple_of(src[j], PROJ_SRC_ALIGN), 0))],
        out_specs=pl.BlockSpec((tm, tn), lambda i, j, src: (i, j)),
    )
    return pl.pallas_call(
        _proj_kernel,
        grid_spec=grid_spec,
        out_shape=jax.ShapeDtypeStruct((m, len(tile_src) * tn), BF16),
        compiler_params=pltpu.CompilerParams(
            dimension_semantics=("parallel", "arbitrary"), vmem_limit_bytes=vmem),
        name=name,
    )(jnp.asarray(tile_src, jnp.int32), xb, wt)


def _proj_cast_kernel(src_ref, x_ref, wt_ref, wb_ref, wo_ref, o_ref, wb16_ref, wo16_ref):
    del src_ref
    o_ref[...] = _dot_nt(x_ref[...], wt_ref[0].astype(BF16)).astype(o_ref.dtype)
    wb16_ref[...] = wb_ref[...].astype(BF16)
    wo16_ref[...] = wo_ref[...].astype(BF16)


def _proj_and_cast(xb, wt, w_branch, w_o, layer):
    m, d = xb.shape
    tm, tn, tile_src = min(PROJ_TILE_M, m), PROJ_TILE_N, PROJ_TILE_SRC
    n_tiles = len(tile_src)
    rows_b, rows_o = w_branch.shape[1], w_o.shape[1]
    steps = (m // tm) * n_tiles
    slab = next(r for r in (16 << k for k in range(12))
                if rows_b % r == 0 and rows_o % r == 0 and steps * r >= max(rows_b, rows_o))
    slab_b = lambda i, j: jnp.minimum(i * n_tiles + j, rows_b // slab - 1)
    slab_o = lambda i, j: jnp.minimum(i * n_tiles + j, rows_o // slab - 1)
    vmem = (2 * (tm * d * 2 + d * tn * 4 + tm * tn * 2) + d * tn * 2 + 4 * 2 * slab * d * 6 + (8 << 20))
    grid_spec = pltpu.PrefetchScalarGridSpec(
        num_scalar_prefetch=1,
        grid=(m // tm, n_tiles),
        in_specs=[pl.BlockSpec((tm, d), lambda i, j, src: (i, 0)),
                  pl.BlockSpec((pl.Element(1), pl.Element(tn), pl.Element(d)),
                               lambda i, j, src: (layer, pl.multiple_of(src[j], PROJ_SRC_ALIGN), 0)),
                  pl.BlockSpec((None, slab, d), lambda i, j, src: (layer, slab_b(i, j), 0)),
                  pl.BlockSpec((None, slab, d), lambda i, j, src: (layer, slab_o(i, j), 0))],
        out_specs=[pl.BlockSpec((tm, tn), lambda i, j, src: (i, j)),
                   pl.BlockSpec((slab, d), lambda i, j, src: (slab_b(i, j), 0)),
                   pl.BlockSpec((slab, d), lambda i, j, src: (slab_o(i, j), 0))],
    )
    return pl.pallas_call(
        _proj_cast_kernel,
        grid_spec=grid_spec,
        out_shape=[jax.ShapeDtypeStruct((m, n_tiles * tn), BF16),
                   jax.ShapeDtypeStruct((rows_b, d), BF16), jax.ShapeDtypeStruct((rows_o, d), BF16)],
        compiler_params=pltpu.CompilerParams(
            dimension_semantics=("arbitrary", "arbitrary"), vmem_limit_bytes=vmem),
        name="in_proj",
    )(jnp.asarray(tile_src, jnp.int32), xb, wt, w_branch, w_o)


GLA_CHUNK_ROWS = 256
GLA_SUB = 16
GLA_FAST_MAX_DECAY = 30.0
GLA_FAST_MIN_Q = 1e-20
GLA_FAST_MAX_QK = 1e20


def _gla_kernel(q_ref, k_ref, v_ref, gate_ref, x_ref, wlr_ref, wup_ref, bias_ref, ng_ref, o_ref,
                s_ref, b_ref, qs_ref, qx_ref, kf_ref, vf_ref, acc_ref, a_ref):
    C, H, dk, dv = GLA_CHUNK_ROWS, GLA_HEADS, GLA_DK, GLA_DV

    @pl.when(pl.program_id(1) == 0)
    def _():
        s_ref[...] = jnp.zeros_like(s_ref)

    alr = _dot_nt(x_ref[...], wlr_ref[0].astype(BF16))
    rank_lane = lax.broadcasted_iota(jnp.int32, alr.shape, 1) < GLA_RANK
    alr = jnp.where(rank_lane, alr, 0.0).astype(BF16)
    z = jnp.dot(alr, wup_ref[...], preferred_element_type=F32) + bias_ref[...]
    g = -(jnp.maximum(-z, 0.0) + jnp.log1p(jnp.exp(-jnp.abs(z)))) * (1.0 / GLA_TAU)

    row = lax.broadcasted_iota(jnp.int32, (C, C), 0)
    col = lax.broadcasted_iota(jnp.int32, (C, C), 1)
    tril = jnp.where(col <= row, 1.0, 0.0).astype(BF16)
    g1 = g.astype(BF16)
    r1 = g - g1.astype(F32)
    g2 = r1.astype(BF16)
    g3 = (r1 - g2.astype(F32)).astype(BF16)
    b_all = (jnp.dot(tril, g1, preferred_element_type=F32)
             + jnp.dot(tril, g2, preferred_element_type=F32)
             + jnp.dot(tril, g3, preferred_element_type=F32))

    q_max = jnp.float32(0.0)
    k_max = jnp.float32(0.0)
    for h in range(H):
        b = b_all[:, h * dk:(h + 1) * dk]
        qs = q_ref[:, h * dk:(h + 1) * dk].astype(F32) * (dk ** -0.5)
        kf = k_ref[:, h * dk:(h + 1) * dk].astype(F32)
        qx = (qs * jnp.exp(b)).astype(BF16)
        b_ref[h] = b
        qs_ref[h] = qs
        kf_ref[h] = kf
        qx_ref[h] = qx
        q_max = jnp.maximum(q_max, jnp.max(jnp.abs(qs)))
        k_max = jnp.maximum(k_max, jnp.max(jnp.abs(kf)))
        acc_ref[h] = jnp.dot(qx, s_ref[h].astype(BF16), preferred_element_type=F32)

    total_decay = jnp.max(-b_all[C - 1:C, :])
    in_range = ((total_decay <= GLA_FAST_MAX_DECAY) & (q_max >= GLA_FAST_MIN_Q)
                & (q_max <= GLA_FAST_MAX_QK) & (k_max <= GLA_FAST_MAX_QK))

    @pl.when(in_range)
    def _():
        for h in range(H):
            k_inv = (kf_ref[h] * jnp.exp(-b_ref[h])).astype(BF16)
            a_ref[h] = jnp.where(col <= row, _dot_nt(qx_ref[h], k_inv), 0.0).astype(BF16)
        for h in range(H):
            acc_ref[h] += jnp.dot(a_ref[h], v_ref[:, h * dv:(h + 1) * dv], preferred_element_type=F32)

    @pl.when(jnp.logical_not(in_range))
    def _():
        _gla_sub_blocks(v_ref, b_ref, qs_ref, kf_ref, vf_ref, acc_ref)

    for h in range(H):
        b = b_ref[h]
        b_end = b[C - 1:C, :]
        k_end = kf_ref[h] * jnp.exp(b_end - b)
        decay_col = jnp.transpose(jnp.broadcast_to(jnp.exp(b_end), (dk, dk)))
        decay = jnp.concatenate([decay_col] * (dv // dk), axis=1)
        s_ref[h] = s_ref[h] * decay + jnp.dot(jnp.transpose(k_end).astype(BF16),
                                              v_ref[:, h * dv:(h + 1) * dv],
                                              preferred_element_type=F32)
        o = acc_ref[h]
        o = o * lax.rsqrt(jnp.mean(o * o, axis=-1, keepdims=True) + RMS_EPS) * ng_ref[...]
        gate = gate_ref[:, h * dv:(h + 1) * dv].astype(F32)
        o_ref[:, h * dv:(h + 1) * dv] = (o * _silu(gate)).astype(o_ref.dtype)


def _gla_sub_blocks(v_ref, b_ref, qs_ref, kf_ref, vf_ref, acc_ref):
    C, H, dk, dv = GLA_CHUNK_ROWS, GLA_HEADS, GLA_DK, GLA_DV
    for h in range(H):
        vf_ref[h] = v_ref[:, h * dv:(h + 1) * dv].astype(F32)
    key_idx = lax.broadcasted_iota(jnp.int32, (GLA_SUB, C), 1)
    sub_row = lax.broadcasted_iota(jnp.int32, (GLA_SUB, dk), 0)

    def sub_block(i, carry):
        base = pl.multiple_of(i * GLA_SUB, GLA_SUB)
        prev = jnp.maximum(base - 1, 0)
        for h in range(H):
            b_start = b_ref[h, pl.ds(prev, 1), :]
            b_i = b_ref[h, pl.ds(base, GLA_SUB), :]
            qs_i = qs_ref[h, pl.ds(base, GLA_SUB), :]
            kx = kf_ref[h] * jnp.exp(jnp.minimum(b_start - b_ref[h], 0.0))
            qx = qs_i * jnp.exp(jnp.minimum(b_i - b_start, 0.0))
            s = _dot_nt(qx.astype(BF16), kx.astype(BF16))
            s = jnp.where(key_idx < base, s, 0.0)
            o_past = jnp.dot(s.astype(BF16), v_ref[:, h * dv:(h + 1) * dv], preferred_element_type=F32)
            terms = []
            for j in range(GLA_SUB):
                b_j = b_ref[h, pl.ds(base + j, 1), :]
                k_j = kf_ref[h, pl.ds(base + j, 1), :]
                v_j = vf_ref[h, pl.ds(base + j, 1), :]
                t = jnp.exp(jnp.minimum(b_i - b_j, 0.0)) * (qs_i * k_j)
                t = jnp.where(sub_row >= j, t, 0.0)
                terms.append(jnp.sum(t, axis=-1, keepdims=True) * v_j)
            while len(terms) > 1:
                terms = [a + b for a, b in zip(terms[0::2], terms[1::2])]
            acc_ref[h, pl.ds(base, GLA_SUB), :] += terms[0] + o_past
        return carry

    lax.fori_loop(0, C // GLA_SUB, sub_block, 0)


def _gla(p, xb, wt, wup, bias, ng, batch, seq, layer):
    C, H, dk, dv = GLA_CHUNK_ROWS, GLA_HEADS, GLA_DK, GLA_DV
    nc = seq // C
    rows = lambda b, c: b * nc + c
    q0, k0 = PROJ_OFF[SEG_AQ] // (H * dk), PROJ_OFF[SEG_AK] // (H * dk)
    v0, g0 = PROJ_OFF[SEG_AV] // (H * dv), PROJ_OFF[SEG_AGATE] // (H * dv)
    d = xb.shape[1]
    return pl.pallas_call(
        _gla_kernel,
        grid=(batch, nc),
        in_specs=[
            pl.BlockSpec((C, H * dk), lambda b, c: (rows(b, c), q0)),
            pl.BlockSpec((C, H * dk), lambda b, c: (rows(b, c), k0)),
            pl.BlockSpec((C, H * dv), lambda b, c: (rows(b, c), v0)),
            pl.BlockSpec((C, H * dv), lambda b, c: (rows(b, c), g0)),
            pl.BlockSpec((C, d), lambda b, c: (rows(b, c), 0)),
            pl.BlockSpec((pl.Element(1), pl.Element(LANES), pl.Element(d)),
                         lambda b, c: (layer, SRC_OFF[SEG_ALR], 0)),
            pl.BlockSpec((None, LANES, H * dk), lambda b, c: (layer, 0, 0)),
            pl.BlockSpec((None, 1, H * dk), lambda b, c: (layer, 0, 0)),
            pl.BlockSpec((None, 1, dv), lambda b, c: (layer, 0, 0)),
        ],
        out_specs=pl.BlockSpec((C, H * dv), lambda b, c: (rows(b, c), 0)),
        out_shape=jax.ShapeDtypeStruct((batch * seq, H * dv), BF16),
        scratch_shapes=[
            pltpu.VMEM((H, dk, dv), F32),
            pltpu.VMEM((H, C, dk), F32),
            pltpu.VMEM((H, C, dk), F32),
            pltpu.VMEM((H, C, dk), BF16),
            pltpu.VMEM((H, C, dk), F32),
            pltpu.VMEM((H, C, dv), F32),
            pltpu.VMEM((H, C, dv), F32),
            pltpu.VMEM((H, C, C), BF16),
        ],
        compiler_params=pltpu.CompilerParams(dimension_semantics=("parallel", "arbitrary")),
        name="gla",
    )(p, p, p, p, xb, wt, wup, bias, ng)


SWA_BLOCKS_PER_STEP = 4
SWA_MASKED = -1e30


def _swa_kernel(layer, sink_ref, q_ref, kp_ref, kc_ref, vp_ref, vc_ref, gate_ref, o_ref, s_ref, p_ref):
    W, hd, NB = SWA_WINDOW, SWA_HEAD_DIM, SWA_BLOCKS_PER_STEP
    group = SWA_Q_HEADS // SWA_KV_HEADS
    pairs = group // 2
    NQ = group * W
    n = pl.program_id(1)
    kk = jnp.concatenate([kp_ref[...], kc_ref[...]], axis=0)
    vv = jnp.concatenate([vp_ref[...], vc_ref[...]], axis=0)
    vt_all = jnp.transpose(vv.astype(F32))
    key_lane = lax.broadcasted_iota(jnp.int32, kk.shape, 1)
    q_lane = lax.broadcasted_iota(jnp.int32, (W, LANES), 1)
    key = lax.broadcasted_iota(jnp.int32, (2 * W, W), 0)
    qry = lax.broadcasted_iota(jnp.int32, (2 * W, W), 1)
    band = (key > qry) & (key <= qry + W)
    head_of_col = lax.broadcasted_iota(jnp.int32, (1, NQ), 1) // W
    swap = lambda t: jnp.concatenate([t[:, hd:], t[:, :hd]], axis=1)
    low_half = lambda t, lane, first: (jnp.where(lane < hd, t, jnp.zeros_like(t)) if first
                                       else swap(jnp.where(lane >= hd, t, jnp.zeros_like(t))))
    c = (hd ** -0.5) * LOG2_E
    inv_scale = float(hd) ** 0.5
    k_low = [low_half(kk, key_lane, hk == 0) for hk in range(SWA_KV_HEADS)]
    for sub in range(NB):
        allowed = band & ((key >= W) | (n * NB + sub > 0))
        q_rows = slice(sub * W, (sub + 1) * W)
        for hk in range(SWA_KV_HEADS):
            slot = sub * SWA_KV_HEADS + hk
            k_lo = k_low[hk][sub * W:(sub + 2) * W]
            vt = vt_all[hk * hd:(hk + 1) * hd, sub * W:(sub + 2) * W].astype(BF16)
            q_stack = jnp.concatenate(
                [low_half(q_ref[q_rows, (hk * group + 2 * pr) * hd:(hk * group + 2 * pr + 2) * hd], q_lane, first)
                 for pr in range(pairs) for first in (True, False)], axis=0)
            sink = jnp.zeros((1, NQ), F32)
            for g in range(group):
                sink = jnp.where(head_of_col == g, sink_ref[layer, hk * group + g] * inv_scale, sink)
            m = sink
            for r in range(2):
                rows = slice(r * W, (r + 1) * W)
                sc = _dot_nt(k_lo[rows], q_stack)
                for g in range(group):
                    cols = slice(g * W, (g + 1) * W)
                    s_ref[slot, rows, cols] = jnp.where(allowed[rows], sc[:, cols], SWA_MASKED)
                m = jnp.maximum(m, jnp.max(s_ref[slot, rows, :], axis=0, keepdims=True))
            den = jnp.exp2((sink - m) * c)
            for r in range(2):
                rows = slice(r * W, (r + 1) * W)
                p = jnp.exp2((s_ref[slot, rows, :] - m) * c)
                den = den + jnp.sum(p, axis=0, keepdims=True)
                p_ref[slot, rows, :] = p.astype(BF16)
            o_t = jnp.dot(vt, p_ref[slot], preferred_element_type=F32) * (1.0 / den)
            for pr in range(pairs):
                col0 = (hk * group + 2 * pr) * hd
                o = jnp.transpose(jnp.concatenate([o_t[:, 2 * pr * W:(2 * pr + 1) * W],
                                                   o_t[:, (2 * pr + 1) * W:(2 * pr + 2) * W]], axis=0))
                gate = gate_ref[q_rows, col0:col0 + LANES].astype(F32)
                o_ref[q_rows, col0:col0 + LANES] = (o * _silu(gate)).astype(o_ref.dtype)


def _swa(p, tail, sinks, batch, seq, layer):
    W, NB = SWA_WINDOW, SWA_BLOCKS_PER_STEP
    steps = seq // (NB * W)
    assert seq % (NB * W) == 0
    width = SWA_Q_HEADS * SWA_HEAD_DIM
    q0, g0 = PROJ_OFF[SEG_BQ] // width, PROJ_OFF[SEG_BGATE] // width
    k0, v0 = TAIL_OFF[SEG_BK] // LANES, TAIL_OFF[SEG_BV] // LANES
    cur = lambda b, n: b * steps + n
    prev = lambda b, n: (b * steps + n) * NB - jnp.minimum(n, 1)
    slots = NB * SWA_KV_HEADS
    nq = SWA_Q_HEADS // SWA_KV_HEADS * W
    return pl.pallas_call(
        functools.partial(_swa_kernel, layer),
        grid=(batch, steps),
        in_specs=[
            pl.BlockSpec(memory_space=pltpu.SMEM),
            pl.BlockSpec((NB * W, width), lambda b, n: (cur(b, n), q0)),
            pl.BlockSpec((W, LANES), lambda b, n: (prev(b, n), k0)),
            pl.BlockSpec((NB * W, LANES), lambda b, n: (cur(b, n), k0)),
            pl.BlockSpec((W, LANES), lambda b, n: (prev(b, n), v0)),
            pl.BlockSpec((NB * W, LANES), lambda b, n: (cur(b, n), v0)),
            pl.BlockSpec((NB * W, width), lambda b, n: (cur(b, n), g0)),
        ],
        out_specs=pl.BlockSpec((NB * W, width), lambda b, n: (cur(b, n), 0)),
        out_shape=jax.ShapeDtypeStruct((batch * seq, width), BF16),
        scratch_shapes=[
            pltpu.VMEM((slots, 2 * W, nq), F32),
            pltpu.VMEM((slots, 2 * W, nq), BF16),
        ],
        compiler_params=pltpu.CompilerParams(dimension_semantics=("parallel", "parallel")),
        name="swa",
    )(sinks, p, tail, tail, tail, tail, p)


MOBA_PENALTY = 1e30
LOG2_E = 1.4426950408889634


MOBA_HEAD_GROUP = 4


MOBA_KEY_CHUNK = 128
MOBA_SUM_ROWS = 16


def _moba_kernel(q_ref, k_ref, v_ref, gate_ref, o_ref,
                 kmean_ref, vt_ref, pen_ref, acc_ref, s_ref, p_ref, so_ref, po_ref):
    BLK, hd, HG, RC = MOBA_BLOCK, MOBA_HEAD_DIM, MOBA_HEAD_GROUP, MOBA_KEY_CHUNK
    PAIR = 2 * BLK
    n_chunks = PAIR // RC
    i = pl.program_id(2)
    nblk = k_ref.shape[0] // BLK
    heads = [slice(h * hd, (h + 1) * hd) for h in range(HG)]
    c = (hd ** -0.5) * LOG2_E

    @pl.when(i == 0)
    def _():
        for h, cols in enumerate(heads):
            kf = k_ref[:, cols].astype(F32).reshape(nblk, BLK, hd)
            rest = jnp.sum(kf, axis=1) * (1.0 / BLK)
            rest = jnp.concatenate([rest, jnp.zeros((LANES - nblk, hd), F32)], axis=0)
            for t in range(3):
                term = rest.astype(BF16)
                kmean_ref[h, t * nblk:(t + 1) * nblk, :] = term[:nblk]
                rest = rest - term.astype(F32)
            extra = lax.broadcasted_iota(jnp.int32, (MOBA_SUM_ROWS, BLK), 0)
            ones_rows = jnp.where(extra == 0, 1.0, 0.0).astype(BF16)
            for n in range(nblk):
                vt = jnp.transpose(v_ref[n * BLK:(n + 1) * BLK, cols].astype(F32)).astype(BF16)
                vt_ref[h, n // 2, :, (n % 2) * BLK:(n % 2 + 1) * BLK] = jnp.concatenate([vt, ones_rows], axis=0)

    def softmax_steps(pair, carry, next_pair=None):
        m_news, alphas = [], []
        for h in range(HG):
            m_prev, m_a, m_b = carry[h]
            pen_a = pen_ref[h, pl.ds(2 * pair, 1), :]
            pen_b = pen_ref[h, pl.ds(2 * pair + 1, 1), :]
            m_new = jnp.maximum(m_prev, jnp.maximum(m_a - pen_a, m_b - pen_b))
            for r in range(n_chunks):
                off = m_new + (pen_a if r < n_chunks // 2 else pen_b)
                p = jnp.exp2((s_ref[h, r * RC:(r + 1) * RC, :] - off) * c)
                p_ref[h, r * RC:(r + 1) * RC, :] = p.astype(BF16)
            m_news.append(m_new)
            alphas.append(jnp.exp2((m_prev - m_new) * c))
        maxes = [()] * HG
        if next_pair is not None:
            start = pl.multiple_of(next_pair * PAIR, PAIR)
            for h in range(HG):
                sc = _dot_nt(k_ref[pl.ds(start, PAIR), heads[h]], q_ref[:, heads[h]])
                s_ref[h] = sc
                maxes[h] = (jnp.max(sc[:BLK], axis=0, keepdims=True), jnp.max(sc[BLK:], axis=0, keepdims=True))
        for h in range(HG):
            acc_ref[h] = alphas[h] * acc_ref[h] + jnp.dot(vt_ref[h, pair], p_ref[h],
                                                          preferred_element_type=F32)
        return tuple((m_news[h],) + maxes[h] for h in range(HG))

    own = pl.multiple_of(i * BLK, BLK)
    key_c = lax.broadcasted_iota(jnp.int32, (RC, BLK), 0)
    qry_c = lax.broadcasted_iota(jnp.int32, (RC, BLK), 1)
    blk = lax.broadcasted_iota(jnp.int32, (nblk, BLK), 0)
    neg_inf = jnp.float32(-jnp.inf)

    gates, own_maxes, pair0_maxes = [], [], []
    for h, cols in enumerate(heads):
        q = q_ref[:, cols]
        lhs = jnp.concatenate([k_ref[pl.ds(own, BLK), cols], kmean_ref[h], k_ref[0:PAIR, cols]], axis=0)
        both = _dot_nt(lhs, q)
        gate = both[BLK:BLK + nblk] + both[BLK + nblk:BLK + 2 * nblk] + both[BLK + 2 * nblk:BLK + 3 * nblk]
        pair0 = both[BLK + 3 * nblk:]
        s_ref[h] = pair0
        pair0_max = (jnp.max(pair0[:BLK], axis=0, keepdims=True), jnp.max(pair0[BLK:], axis=0, keepdims=True))
        own_max = []
        for r in range(BLK // RC):
            rows = slice(r * RC, (r + 1) * RC)
            sc = jnp.where(key_c + r * RC <= qry_c, both[rows], -MOBA_PENALTY)
            so_ref[h, rows, :] = sc
            own_max.append(jnp.max(sc, axis=0, keepdims=True))
        gates.append(gate)
        own_maxes.append(functools.reduce(jnp.maximum, own_max))
        pair0_maxes.append(pair0_max)

    carry = []
    for h, cols in enumerate(heads):
        g = jnp.where(blk < i, gates[h], neg_inf)
        sel = jnp.zeros(g.shape, jnp.bool_)
        for _ in range(MOBA_TOPK):
            m = jnp.max(g, axis=0, keepdims=True)
            idx = jnp.min(jnp.where(g == m, blk, nblk), axis=0, keepdims=True)
            pick = (blk == idx) & (m > neg_inf)
            sel = sel | pick
            g = jnp.where(pick, neg_inf, g)
        pen_ref[h] = jnp.where(sel, 0.0, MOBA_PENALTY)

        m0 = own_maxes[h]
        for r in range(BLK // RC):
            rows = slice(r * RC, (r + 1) * RC)
            po_ref[h, rows, :] = jnp.exp2((so_ref[h, rows, :] - m0) * c).astype(BF16)
        own_vt = vt_ref[h, i // 2, :, pl.ds(pl.multiple_of((i % 2) * BLK, BLK), BLK)]
        acc_ref[h] = jnp.dot(own_vt, po_ref[h], preferred_element_type=F32)
        carry.append((m0,) + pair0_maxes[h])

    n_pairs = (i + 1) // 2

    carry = lax.fori_loop(0, jnp.maximum(n_pairs - 1, 0),
                          lambda j, carry: softmax_steps(j, carry, next_pair=j + 1), tuple(carry))
    softmax_steps(jnp.maximum(n_pairs - 1, 0), carry)
    for h, cols in enumerate(heads):
        o = jnp.transpose(acc_ref[h, 0:hd, :] * (1.0 / acc_ref[h, hd:hd + 1, :]))
        o_ref[:, cols] = (o * _silu(gate_ref[:, cols].astype(F32))).astype(o_ref.dtype)


def _moba(p, batch, seq):
    BLK, hd, HG = MOBA_BLOCK, MOBA_HEAD_DIM, MOBA_HEAD_GROUP
    nblk = seq // BLK
    assert seq % (2 * BLK) == 0 and MOBA_HEADS % HG == 0
    gw = HG * hd
    q0, k0 = PROJ_OFF[SEG_CQ] // gw, PROJ_OFF[SEG_CK] // gw
    v0, g0 = PROJ_OFF[SEG_CV] // gw, PROJ_OFF[SEG_CGATE] // gw
    scratch_bytes = HG * (seq * hd * 2 + hd * BLK * 4 + 2 * BLK * BLK * 4 + 2 * BLK * BLK * 2)
    vmem = 2 * (2 * seq * gw * 2 + 3 * BLK * gw * 2) + scratch_bytes + (12 << 20)
    return pl.pallas_call(
        _moba_kernel,
        grid=(batch, MOBA_HEADS // HG, nblk),
        in_specs=[
            pl.BlockSpec((BLK, gw), lambda b, h, i: (b * nblk + i, q0 + h)),
            pl.BlockSpec((seq, gw), lambda b, h, i: (b, k0 + h)),
            pl.BlockSpec((seq, gw), lambda b, h, i: (b, v0 + h)),
            pl.BlockSpec((BLK, gw), lambda b, h, i: (b * nblk + i, g0 + h)),
        ],
        out_specs=pl.BlockSpec((BLK, gw), lambda b, h, i: (b * nblk + i, h)),
        out_shape=jax.ShapeDtypeStruct((batch * seq, MOBA_HEADS * hd), BF16),
        scratch_shapes=[
            pltpu.VMEM((HG, 3 * nblk, hd), BF16),
            pltpu.VMEM((HG, nblk // 2, hd + MOBA_SUM_ROWS, 2 * BLK), BF16),
            pltpu.VMEM((HG, nblk, BLK), F32),
            pltpu.VMEM((HG, hd + MOBA_SUM_ROWS, BLK), F32),
            pltpu.VMEM((HG, 2 * BLK, BLK), F32),
            pltpu.VMEM((HG, 2 * BLK, BLK), BF16),
            pltpu.VMEM((HG, BLK, BLK), F32),
            pltpu.VMEM((HG, BLK, BLK), BF16),
        ],
        compiler_params=pltpu.CompilerParams(
            dimension_semantics=("parallel", "parallel", "arbitrary"), vmem_limit_bytes=vmem),
        name="moba",
    )(p, p, p, p)


MERGE_TILE_M = 256


def _merge_kernel(alpha, ya_ref, yb_ref, yc_ref, mg_ref, x_ref, wb_ref, bm_ref, wo_ref, lg_ref, lb_ref,
                  y_ref, yb16_ref):
    D = D_MODEL
    merged = None
    for n, y_n in enumerate((ya_ref, yb_ref, yc_ref)):
        up = jnp.dot(y_n[...], wb_ref[n], preferred_element_type=F32)
        gate = _sigmoid(mg_ref[:, n * D:(n + 1) * D].astype(F32) + bm_ref[n:n + 1, :])
        merged = gate * up if merged is None else merged + gate * up
    out = jnp.dot(merged.astype(BF16), wo_ref[...], preferred_element_type=F32)
    h = alpha * x_ref[...] + out
    mu = jnp.mean(h, axis=-1, keepdims=True)
    hc = h - mu
    var = jnp.mean(hc * hc, axis=-1, keepdims=True)
    y = hc * lax.rsqrt(var + LN_EPS) * lg_ref[...] + lb_ref[...]
    y_ref[...] = y
    yb16_ref[...] = y.astype(BF16)


def _merge(ya, yb, yc, p, x, wb, bm, wo, lg, lb, alpha, layer):
    m = x.shape[0]
    D, Wd = D_MODEL, BRANCH_WIDTH
    tm = min(MERGE_TILE_M, m)
    once = pl.Buffered(1)
    resident = (N_BRANCH * Wd * D + D * D) * 2
    streamed = 2 * (3 * tm * Wd * 2 + tm * 3 * D * 2 + tm * D * 4 + tm * D * 4 + tm * D * 2)
    vmem = resident + streamed + (12 << 20)
    row = lambda i: (i, 0)
    return pl.pallas_call(
        functools.partial(_merge_kernel, alpha),
        grid=(m // tm,),
        in_specs=[
            pl.BlockSpec((tm, Wd), row), pl.BlockSpec((tm, Wd), row), pl.BlockSpec((tm, Wd), row),
            pl.BlockSpec((tm, N_BRANCH * D), lambda i: (i, PROJ_OFF[SEG_MGATE] // (N_BRANCH * D))),
            pl.BlockSpec((tm, D), row),
            pl.BlockSpec((N_BRANCH, Wd, D), lambda i: (0, 0, 0), pipeline_mode=once),
            pl.BlockSpec((None, N_BRANCH, D), lambda i: (layer, 0, 0)),
            pl.BlockSpec((D, D), lambda i: (0, 0), pipeline_mode=once),
            pl.BlockSpec((None, 1, D), lambda i: (layer, 0, 0)),
            pl.BlockSpec((None, 1, D), lambda i: (layer, 0, 0)),
        ],
        out_specs=[pl.BlockSpec((tm, D), row), pl.BlockSpec((tm, D), row)],
        out_shape=[jax.ShapeDtypeStruct((m, D), F32), jax.ShapeDtypeStruct((m, D), BF16)],
        compiler_params=pltpu.CompilerParams(
            dimension_semantics=("parallel",), vmem_limit_bytes=min(vmem, V7X_VMEM_BYTES - (4 << 20))),
        name="merge_out_ln",
    )(ya, yb, yc, p, x, wb, bm, wo, lg, lb)


def kernel(x, w_in, gla_w_up, gla_b, gla_norm_g, swa_sinks, b_merge, w_branch, w_o, ln_g, ln_b):
    batch, seq, d = x.shape
    depth = w_in.shape[0]
    alpha = (2 * depth) ** 0.25
    assert d == D_MODEL and seq % (2 * MOBA_BLOCK) == 0 and seq % GLA_CHUNK_ROWS == 0

    w_in_t = jnp.swapaxes(w_in, 1, 2)
    wup = jnp.pad(gla_w_up, ((0, 0), (0, LANES - GLA_RANK), (0, 0))).astype(BF16)
    w_branch_rows = w_branch.reshape(depth, N_BRANCH * BRANCH_WIDTH, d)
    gla_bias, gla_ng = gla_b[:, None, :], gla_norm_g[:, None, :]
    lg, lb = ln_g[:, None, :], ln_b[:, None, :]

    xf = x.reshape(batch * seq, d)
    xb = xf.astype(BF16)
    for l in range(depth):
        p, wb, wo = _proj_and_cast(xb, w_in_t, w_branch_rows, w_o, l)
        tail = _proj(xb, w_in_t, l, TAIL_TILE_SRC, TAIL_TILE_N, TAIL_TILE_M, "in_proj_tail")
        ya = _gla(p, xb, w_in_t, wup, gla_bias, gla_ng, batch, seq, l)
        yb = _swa(p, tail, swa_sinks, batch, seq, l)
        yc = _moba(p, batch, seq)
        xf, xb = _merge(ya, yb, yc, p, xf, wb.reshape(N_BRANCH, BRANCH_WIDTH, d), b_merge, wo, lg, lb, alpha, l)
    return xf.reshape(batch, seq, d)
```
